```python
import jax, jax.numpy as jnp
from jax import lax
import numpy as np

D_MODEL = 1024
BATCH = 2
SEQ = 8192
DEPTH = 2

CONV_W = D_MODEL // 2
CONV_K = 31
CONV_LN_EPS = 1e-5
RWKV_HEAD_DIM = 64
RWKV_HEADS = (D_MODEL // 2) // RWKV_HEAD_DIM
RWKV_W = RWKV_HEADS * RWKV_HEAD_DIM
LORA_DECAY = 64
LORA_ICLR = 64
LORA_GATE = 128
RWKV_GN_EPS = RWKV_HEAD_DIM * 1e-5
RWKV_IN = 3 * RWKV_W + LORA_DECAY + LORA_ICLR + LORA_GATE
RWKV_SPLITS = (RWKV_W, 2 * RWKV_W, 3 * RWKV_W, 3 * RWKV_W + LORA_DECAY, 3 * RWKV_W + LORA_DECAY + LORA_ICLR)
AB_IN = 2 * CONV_W + RWKV_IN
AB_OUT = CONV_W + RWKV_W
HEAD_DIM = 64
N_HEADS = D_MODEL // HEAD_DIM
N_KV_HEADS = 4
GROUP = N_HEADS // N_KV_HEADS
WINDOW = 128
BLOCK = 128
ROT_DIM = HEAD_DIM // 4
ROPE_THETA = 500000.0
QKV_W = (N_HEADS + 2 * N_KV_HEADS) * HEAD_DIM
D_FF = 2816
FFN_CONV_K = 3
NORM_EPS = 1e-6
N_EVEN = (DEPTH + 1) // 2
N_ODD = DEPTH // 2

kernel_name = 'hybrid_conformer_rwkv7_swa_sink_convffn'


def rms_norm(x, g):
    xf = x.astype(jnp.float32)
    y = xf * lax.rsqrt(jnp.mean(xf * xf, axis=-1, keepdims=True) + NORM_EPS)
    return (y * g.astype(jnp.float32)).astype(x.dtype)


def layer_norm(x, g, b, eps):
    xf = x.astype(jnp.float32)
    mu = jnp.mean(xf, axis=-1, keepdims=True)
    xc = xf - mu
    var = jnp.mean(xc * xc, axis=-1, keepdims=True)
    return (xc * lax.rsqrt(var + eps) * g.astype(jnp.float32) + b.astype(jnp.float32)).astype(x.dtype)


def causal_dwconv(x, w, b):
    k = w.shape[0]
    y = lax.conv_general_dilated(x, w[:, None, :].astype(x.dtype), window_strides=(1,),
                                 padding=[(k - 1, 0)], dimension_numbers=('NWC', 'WIO', 'NWC'),
                                 feature_group_count=x.shape[-1])
    return y + b


def token_shift(p):
    return jnp.pad(p, ((0, 0), (1, 0), (0, 0)))[:, :-1]


def partial_rope(x, positions):
    half = ROT_DIM // 2
    inv_freq = ROPE_THETA ** (-(jnp.arange(half, dtype=jnp.float32) * 2.0) / ROT_DIM)
    ang = positions.astype(jnp.float32)[..., None] * inv_freq
    cos = jnp.cos(ang)[:, :, None, :]
    sin = jnp.sin(ang)[:, :, None, :]
    xf = x.astype(jnp.float32)
    x1 = xf[..., :half]
    x2 = xf[..., half:ROT_DIM]
    out = jnp.concatenate([x1 * cos - x2 * sin, x2 * cos + x1 * sin, xf[..., ROT_DIM:]], axis=-1)
    return out.astype(x.dtype)


def rwkv7_recurrence(r, w, k, v, a, b):
    bsz, _, h, n = r.shape

    def step(s, inp):
        r_t, w_t, k_t, v_t, a_t, b_t = inp
        sa = jnp.einsum('bhij,bhj->bhi', s, a_t)
        s = s * w_t[:, :, None, :] + sa[..., None] * b_t[:, :, None, :] + v_t[..., None] * k_t[:, :, None, :]
        return s, jnp.einsum('bhij,bhj->bhi', s, r_t)

    xs = tuple(jnp.moveaxis(t, 1, 0) for t in (r, w, k, v, a, b))
    s0 = jnp.zeros((bsz, h, n, n), jnp.float32)
    _, ys = lax.scan(step, s0, xs)
    return jnp.moveaxis(ys, 0, 1)


def conv_rwkv_mixer(x, norm_g, w_in, conv_in_b, conv_dw_w, conv_dw_b, conv_ln_g, conv_ln_b,
                    rwkv_mu, rwkv_w0, rwkv_w2, rwkv_a0, rwkv_a2, rwkv_g2, rwkv_k_k, rwkv_k_a,
                    rwkv_r_k, rwkv_ln_g, rwkv_ln_b, w_out):
    bsz, t, _ = x.shape
    f32 = jnp.float32
    h = rms_norm(x, norm_g)
    p = h @ w_in
    c = p[..., :2 * CONV_W] + conv_in_b
    c = c[..., :CONV_W] * jax.nn.sigmoid(c[..., CONV_W:])
    c = causal_dwconv(c, conv_dw_w, conv_dw_b)
    c = jax.nn.silu(layer_norm(c, conv_ln_g, conv_ln_b, CONV_LN_EPS))
    rw = p[..., 2 * CONV_W:]
    rw = (rw + (token_shift(rw) - rw) * rwkv_mu).astype(f32)
    r, k, v, wd, ad, gd = jnp.split(rw, RWKV_SPLITS, axis=-1)
    w = -jax.nn.softplus(-(rwkv_w0 + jnp.tanh(wd) @ rwkv_w2)) - 0.5
    decay = jnp.exp(-jnp.exp(w))
    a = jax.nn.sigmoid(rwkv_a0 + ad @ rwkv_a2)
    gate = jax.nn.sigmoid(gd) @ rwkv_g2
    hs = lambda z: z.reshape(bsz, t, RWKV_HEADS, RWKV_HEAD_DIM)
    kk = hs(k * rwkv_k_k)
    kk = kk / jnp.maximum(jnp.sqrt(jnp.sum(kk * kk, axis=-1, keepdims=True)), 1e-12)
    k = k * (1.0 + (a - 1.0) * rwkv_k_a)
    rh, kh, vh = hs(r), hs(k), hs(v)
    y = rwkv7_recurrence(rh, hs(decay), kh, vh, -kk, kk * hs(a))
    mu = jnp.mean(y, axis=-1, keepdims=True)
    yc = y - mu
    y = yc * lax.rsqrt(jnp.mean(yc * yc, axis=-1, keepdims=True) + RWKV_GN_EPS)
    ln_g = rwkv_ln_g.astype(f32).reshape(RWKV_HEADS, RWKV_HEAD_DIM)
    ln_b = rwkv_ln_b.astype(f32).reshape(RWKV_HEADS, RWKV_HEAD_DIM)
    y = y * ln_g + ln_b
    y = y + jnp.sum(rh * kh * rwkv_r_k, axis=-1, keepdims=True) * vh
    y = y.reshape(bsz, t, RWKV_W) * gate
    return jnp.concatenate([c, y.astype(c.dtype)], axis=-1) @ w_out


def sliding_window_sink_attention(x, positions, norm_g, w_qkv, b_qkv, q_norm_g, k_norm_g, sinks, w_o, b_o):
    bsz, t, _ = x.shape
    nb = t // BLOCK
    h = rms_norm(x, norm_g)
    qkv = h @ w_qkv + b_qkv
    qd, kd = N_HEADS * HEAD_DIM, N_KV_HEADS * HEAD_DIM
    q = qkv[..., :qd].reshape(bsz, t, N_HEADS, HEAD_DIM)
    k = qkv[..., qd:qd + kd].reshape(bsz, t, N_KV_HEADS, HEAD_DIM)
    v = qkv[..., qd + kd:].reshape(bsz, t, N_KV_HEADS, HEAD_DIM)
    q = partial_rope(rms_norm(q, q_norm_g), positions)
    k = partial_rope(rms_norm(k, k_norm_g), positions)
    q = q.reshape(bsz, nb, BLOCK, N_KV_HEADS, GROUP, HEAD_DIM)

    def with_prev(z):
        zb = z.reshape(bsz, nb, BLOCK, N_KV_HEADS, HEAD_DIM)
        prev = jnp.pad(zb, ((0, 0), (1, 0), (0, 0), (0, 0), (0, 0)))[:, :-1]
        return jnp.concatenate([prev, zb], axis=2)

    kc, vc = with_prev(k), with_prev(v)
    s = jnp.einsum('bnqhgd,bnkhd->bnhgqk', q, kc).astype(jnp.float32) * (HEAD_DIM ** -0.5)
    qi = jnp.arange(BLOCK)[:, None]
    kj = jnp.arange(2 * BLOCK)[None, :]
    rel = qi + BLOCK - kj
    band = (rel >= 0) & (rel < WINDOW)
    kpos = jnp.arange(nb)[:, None, None] * BLOCK - BLOCK + kj[None]
    valid = band[None] & (kpos >= 0)
    s = jnp.where(valid[None, :, None, None], s, -jnp.inf)
    sink = sinks.astype(jnp.float32).reshape(N_KV_HEADS, GROUP)[None, None, :, :, None, None]
    m = jnp.maximum(jnp.max(s, axis=-1, keepdims=True), sink)
    pr = jnp.exp(s - m)
    pr = pr / (jnp.sum(pr, axis=-1, keepdims=True) + jnp.exp(sink - m))
    o = jnp.einsum('bnhgqk,bnkhd->bnqhgd', pr.astype(vc.dtype), vc)
    return o.reshape(bsz, t, N_HEADS * HEAD_DIM) @ w_o + b_o


def conv_glu_ffn(x, norm_g, w_up, conv_w, conv_b, w_down):
    h = rms_norm(x, norm_g)
    u = h @ w_up
    gate = causal_dwconv(u[..., :D_FF], conv_w, conv_b)
    return (jax.nn.silu(gate) * u[..., D_FF:]) @ w_down


def setup_inputs(seed: int = 0) -> dict:
    key = jax.random.key(seed)
    ks = iter(jax.random.split(key, 40))
    f32 = jnp.float32

    def nrm(shape, scale):
        return scale * jax.random.normal(next(ks), shape, f32)

    def gain(shape):
        return 1.0 + nrm(shape, 0.02)

    e, o, l = N_EVEN, N_ODD, DEPTH
    return {
        'x': jax.random.normal(next(ks), (BATCH, SEQ, D_MODEL), f32),
        'positions': jnp.broadcast_to(jnp.arange(SEQ, dtype=jnp.int32), (BATCH, SEQ)),
        'ab_norm_g': gain((e, D_MODEL)),
        'ab_w_in': nrm((e, D_MODEL, AB_IN), D_MODEL ** -0.5),
        'conv_in_b': nrm((e, 2 * CONV_W), 0.02),
        'conv_dw_w': nrm((e, CONV_K, CONV_W), CONV_K ** -0.5),
        'conv_dw_b': nrm((e, CONV_W), 0.02),
        'conv_ln_g': gain((e, CONV_W)),
        'conv_ln_b': nrm((e, CONV_W), 0.02),
        'rwkv_mu': jax.random.uniform(next(ks), (e, RWKV_IN), f32),
        'rwkv_w0': jax.random.uniform(next(ks), (e, RWKV_W), f32, -6.0, -1.0),
        'rwkv_w2': nrm((e, LORA_DECAY, RWKV_W), 0.1 * LORA_DECAY ** -0.5),
        'rwkv_a0': nrm((e, RWKV_W), 0.1),
        'rwkv_a2': nrm((e, LORA_ICLR, RWKV_W), 0.3 * LORA_ICLR ** -0.5),
        'rwkv_g2': nrm((e, LORA_GATE, RWKV_W), LORA_GATE ** -0.5),
        'rwkv_k_k': 0.85 + nrm((e, RWKV_W), 0.1),
        'rwkv_k_a': 1.0 + nrm((e, RWKV_W), 0.1),
        'rwkv_r_k': nrm((e, RWKV_HEADS, RWKV_HEAD_DIM), 0.1),
        'rwkv_ln_g': gain((e, RWKV_W)),
        'rwkv_ln_b': nrm((e, RWKV_W), 0.02),
        'ab_w_out': nrm((e, AB_OUT, D_MODEL), 0.5 * AB_OUT ** -0.5),
        'attn_norm_g': gain((o, D_MODEL)),
        'attn_w_qkv': nrm((o, D_MODEL, QKV_W), D_MODEL ** -0.5),
        'attn_b_qkv': nrm((o, QKV_W), 0.02),
        'attn_q_norm_g': gain((o, HEAD_DIM)),
        'attn_k_norm_g': gain((o, HEAD_DIM)),
        'attn_sinks': nrm((o, N_HEADS), 0.5),
        'attn_w_o': nrm((o, N_HEADS * HEAD_DIM, D_MODEL), 0.5 * (N_HEADS * HEAD_DIM) ** -0.5),
        'attn_b_o': nrm((o, D_MODEL), 0.02),
        'ffn_norm_g': gain((l, D_MODEL)),
        'ffn_w_up': nrm((l, D_MODEL, 2 * D_FF), D_MODEL ** -0.5),
        'ffn_conv_w': nrm((l, FFN_CONV_K, D_FF), FFN_CONV_K ** -0.5),
        'ffn_conv_b': nrm((l, D_FF), 0.02),
        'ffn_w_down': nrm((l, D_FF, D_MODEL), 0.5 * D_FF ** -0.5),
    }


def reference(x, positions, ab_norm_g, ab_w_in, conv_in_b, conv_dw_w, conv_dw_b, conv_ln_g, conv_ln_b,
              rwkv_mu, rwkv_w0, rwkv_w2, rwkv_a0, rwkv_a2, rwkv_g2, rwkv_k_k, rwkv_k_a, rwkv_r_k,
              rwkv_ln_g, rwkv_ln_b, ab_w_out, attn_norm_g, attn_w_qkv, attn_b_qkv, attn_q_norm_g,
              attn_k_norm_g, attn_sinks, attn_w_o, attn_b_o, ffn_norm_g, ffn_w_up, ffn_conv_w,
              ffn_conv_b, ffn_w_down):
    for layer in range(DEPTH):
        i = layer // 2
        if layer % 2 == 0:
            mix = conv_rwkv_mixer(x, ab_norm_g[i], ab_w_in[i], conv_in_b[i], conv_dw_w[i], conv_dw_b[i],
                                  conv_ln_g[i], conv_ln_b[i], rwkv_mu[i], rwkv_w0[i], rwkv_w2[i],
                                  rwkv_a0[i], rwkv_a2[i], rwkv_g2[i], rwkv_k_k[i], rwkv_k_a[i],
                                  rwkv_r_k[i], rwkv_ln_g[i], rwkv_ln_b[i], ab_w_out[i])
        else:
            mix = sliding_window_sink_attention(x, positions, attn_norm_g[i], attn_w_qkv[i], attn_b_qkv[i],
                                                attn_q_norm_g[i], attn_k_norm_g[i], attn_sinks[i],
                                                attn_w_o[i], attn_b_o[i])
        x = x + mix.astype(x.dtype)
        x = x + conv_glu_ffn(x, ffn_norm_g[layer], ffn_w_up[layer], ffn_conv_w[layer],
                             ffn_conv_b[layer], ffn_w_down[layer]).astype(x.dtype)
    return x
```

```python
import functools

import jax
import jax.numpy as jnp
from jax import lax
from jax.experimental import pallas as pl
from jax.experimental.pallas import tpu as pltpu

F32 = jnp.float32
BF16 = jnp.bfloat16

CONV_W = 512
CONV_K = 31
CONV_LN_EPS = 1e-5
RWKV_HEADS = 8
RWKV_HEAD_DIM = 64
RWKV_W = RWKV_HEADS * RWKV_HEAD_DIM
LORA_DECAY = 64
LORA_ICLR = 64
LORA_GATE = 128
RWKV_GN_EPS = RWKV_HEAD_DIM * 1e-5
RWKV_IN = 3 * RWKV_W + LORA_DECAY + LORA_ICLR + LORA_GATE
HEAD_DIM = 64
N_HEADS = 16
N_KV_HEADS = 4
GROUP = N_HEADS // N_KV_HEADS
WINDOW = 128
ROT_DIM = 16
ROPE_THETA = 500000.0
D_FF = 2816
FFN_CONV_K = 3
NORM_EPS = 1e-6

CHUNK = 64
CORE_CHUNKS = 2
SUBLANES = 8
CONV_HALO = 32
V7X_VMEM_LIMIT = 56 * 1024 * 1024

NT_DIMS = (((1,), (1,)), ((), ()))
TN_DIMS = (((0,), (0,)), ((), ()))


def _params(n_axes=1):
    return pltpu.CompilerParams(dimension_semantics=("arbitrary",) * n_axes,
                                vmem_limit_bytes=V7X_VMEM_LIMIT)


def _const_spec(shape):
    nd = len(shape)
    return pl.BlockSpec(shape, lambda *_: (0,) * nd, pipeline_mode=pl.Buffered(1))


def _rms_norm(x, g):
    return x * lax.rsqrt(jnp.mean(x * x, axis=-1, keepdims=True) + NORM_EPS) * g


def _sigmoid(z):
    return 1.0 / (1.0 + jnp.exp(-z))


def _split3(z):
    hi = z.astype(BF16)
    r1 = z - hi.astype(F32)
    mid = r1.astype(BF16)
    lo = (r1 - mid.astype(F32)).astype(BF16)
    return hi, mid, lo


def _inproj_kernel(x_ref, g_ref, w_ref, pc_ref, pr_ref):
    h = _rms_norm(x_ref[...], g_ref[...])
    p = jnp.dot(h.astype(BF16), w_ref[...], preferred_element_type=F32)
    pc_ref[...] = p[:, :2 * CONV_W]
    pr_ref[...] = p[:, 2 * CONV_W:]


def _inproj(x2, g, w, tt):
    n, d = x2.shape
    return pl.pallas_call(
        _inproj_kernel,
        grid=(n // tt,),
        in_specs=[pl.BlockSpec((tt, d), lambda i: (i, 0)), _const_spec(g.shape), _const_spec(w.shape)],
        out_specs=[pl.BlockSpec((tt, 2 * CONV_W), lambda i: (i, 0)),
                   pl.BlockSpec((tt, RWKV_IN), lambda i: (i, 0))],
        out_shape=[jax.ShapeDtypeStruct((n, 2 * CONV_W), F32), jax.ShapeDtypeStruct((n, RWKV_IN), F32)],
        compiler_params=_params(),
    )(x2, g, w)


def _convmod_kernel(pc_ref, halo_ref, inb_ref, dww_ref, dwb_ref, lng_ref, lnb_ref, c_ref, ext_ref,
                    *, tiles_per_seq, row_block):
    tt = pc_ref.shape[0]

    def glu(z):
        z = z + inb_ref[...]
        return z[:, :CONV_W] * _sigmoid(z[:, CONV_W:])

    keep = (pl.program_id(0) % tiles_per_seq != 0).astype(F32)
    ext_ref[0:CONV_HALO, :] = glu(halo_ref[...]) * keep
    ext_ref[CONV_HALO:, :] = glu(pc_ref[...])
    first = CONV_HALO - (CONV_K - 1)
    for r0 in range(0, tt, row_block):
        acc = jnp.broadcast_to(dwb_ref[...], (row_block, CONV_W))
        for j in range(CONV_K):
            acc = acc + dww_ref[j:j + 1, :] * ext_ref[pl.ds(r0 + first + j, row_block), :]
        mu = jnp.mean(acc, axis=-1, keepdims=True)
        xc = acc - mu
        var = jnp.mean(xc * xc, axis=-1, keepdims=True)
        z = xc * lax.rsqrt(var + CONV_LN_EPS) * lng_ref[...] + lnb_ref[...]
        c_ref[r0:r0 + row_block, :] = (z * _sigmoid(z)).astype(c_ref.dtype)


def _convmod(pc, inb, dww, dwb, lng, lnb, seq, tt):
    n = pc.shape[0]
    tps = seq // tt
    hb = tt // CONV_HALO
    kern = functools.partial(_convmod_kernel, tiles_per_seq=tps, row_block=64)
    return pl.pallas_call(
        kern,
        grid=(n // tt,),
        in_specs=[pl.BlockSpec((tt, 2 * CONV_W), lambda i: (i, 0)),
                  pl.BlockSpec((CONV_HALO, 2 * CONV_W), lambda i: (jnp.maximum(i * hb - 1, 0), 0)),
                  _const_spec(inb.shape), _const_spec(dww.shape), _const_spec(dwb.shape),
                  _const_spec(lng.shape), _const_spec(lnb.shape)],
        out_specs=pl.BlockSpec((tt, CONV_W), lambda i: (i, 0)),
        out_shape=jax.ShapeDtypeStruct((n, CONV_W), BF16),
        scratch_shapes=[pltpu.VMEM((tt + CONV_HALO, CONV_W), F32)],
        compiler_params=_params(),
    )(pc, pc, inb, dww, dwb, lng, lnb)


def _rwkv_prep_kernel(pr_ref, halo_ref, mu_ref, w0_ref, w2a2_ref, a0_ref, g2_ref, kk_ref, ka_ref, rk_ref,
                      ones_ref, tri_ref,
                      ar_ref, bk_ref, v_ref, wc_ref, gate_ref, bonus_ref, ext_ref, *, tiles_per_seq):
    tt = pr_ref.shape[0]
    keep = (pl.program_id(0) % tiles_per_seq != 0).astype(F32)
    ext_ref[0:SUBLANES, :] = halo_ref[...] * keep
    ext_ref[SUBLANES:, :] = pr_ref[...]
    rw = pr_ref[...]
    rw = rw + (ext_ref[pl.ds(SUBLANES - 1, tt), :] - rw) * mu_ref[...]

    r = rw[:, 0:RWKV_W]
    k = rw[:, RWKV_W:2 * RWKV_W]
    v = rw[:, 2 * RWKV_W:3 * RWKV_W]
    wa = rw[:, 3 * RWKV_W:3 * RWKV_W + LORA_DECAY + LORA_ICLR]
    gd = rw[:, 3 * RWKV_W + LORA_DECAY + LORA_ICLR:]

    lane = lax.broadcasted_iota(jnp.int32, wa.shape, 1)
    z = jnp.where(lane < LORA_DECAY, jnp.tanh(wa), wa)
    proj = jnp.dot(z.astype(BF16), w2a2_ref[...], preferred_element_type=F32)
    zw = -(w0_ref[...] + proj[:, :RWKV_W])
    softplus = jnp.maximum(zw, 0.0) + jnp.log(1.0 + jnp.exp(-jnp.abs(zw)))
    lw = -jnp.exp(-softplus - 0.5)
    a = _sigmoid(a0_ref[...] + proj[:, RWKV_W:])
    gate_ref[...] = jnp.dot(_sigmoid(gd).astype(BF16), g2_ref[...], preferred_element_type=F32)

    kk = k * kk_ref[...]
    ss = jnp.dot((kk * kk).astype(BF16), ones_ref[...], preferred_element_type=F32)
    kk = kk / jnp.maximum(jnp.sqrt(ss), 1e-12)
    k = k * (1.0 + (a - 1.0) * ka_ref[...])
    rk = jnp.dot((r * k * rk_ref[...]).astype(BF16), ones_ref[...], preferred_element_type=F32)
    bonus_ref[...] = rk * v

    half = tri_ref.shape[0]
    cums = []
    for r0 in range(0, tt, half):
        parts = _split3(lw[r0:r0 + half])
        cums.append(sum(jnp.dot(tri_ref[...], p, preferred_element_type=F32) for p in parts))
    cum = jnp.concatenate(cums, axis=0)
    e_cum = jnp.exp(cum)
    e_inv = jnp.exp(-cum)
    rt = r * e_cum
    at = -kk * jnp.exp(cum - lw)
    kt = k * e_inv
    bt = kk * a * e_inv

    nch = tt // CHUNK
    for h in range(RWKV_HEADS):
        sl = slice(h * RWKV_HEAD_DIM, (h + 1) * RWKV_HEAD_DIM)
        for c in range(nch):
            rows = slice(c * CHUNK, (c + 1) * CHUNK)
            ar_ref[0, h, c, 0:CHUNK, :] = at[rows, sl].astype(BF16)
            ar_ref[0, h, c, CHUNK:, :] = rt[rows, sl].astype(BF16)
            bk_ref[0, h, c, 0:CHUNK, :] = bt[rows, sl].astype(BF16)
            bk_ref[0, h, c, CHUNK:, :] = kt[rows, sl].astype(BF16)
            v_ref[0, h, c, :, :] = v[rows, sl].astype(BF16)
            wc_ref[0, h, c, :, :] = e_cum[(c + 1) * CHUNK - 1:(c + 1) * CHUNK, sl]


def _rwkv_prep(pr, mu, w0, w2a2, a0, g2, k_k, k_a, r_k, ones, tri, bsz, seq, tt):
    n = pr.shape[0]
    tps = seq // tt
    hb = tt // SUBLANES
    nch = tt // CHUNK
    nc = seq // CHUNK
    kern = functools.partial(_rwkv_prep_kernel, tiles_per_seq=tps)
    hm = lambda rows: pl.BlockSpec((1, RWKV_HEADS, nch, rows, RWKV_HEAD_DIM),
                                   lambda i: (i // tps, 0, i % tps, 0, 0))
    hm_shape = lambda rows, dt: jax.ShapeDtypeStruct((bsz, RWKV_HEADS, nc, rows, RWKV_HEAD_DIM), dt)
    consts = [mu, w0, w2a2, a0, g2, k_k, k_a, r_k, ones, tri]
    return pl.pallas_call(
        kern,
        grid=(n // tt,),
        in_specs=[pl.BlockSpec((tt, RWKV_IN), lambda i: (i, 0)),
                  pl.BlockSpec((SUBLANES, RWKV_IN), lambda i: (jnp.maximum(i * hb - 1, 0), 0))]
                 + [_const_spec(c.shape) for c in consts],
        out_specs=[hm(2 * CHUNK), hm(2 * CHUNK), hm(CHUNK), hm(1),
                   pl.BlockSpec((tt, RWKV_W), lambda i: (i, 0)),
                   pl.BlockSpec((tt, RWKV_W), lambda i: (i, 0))],
        out_shape=[hm_shape(2 * CHUNK, BF16), hm_shape(2 * CHUNK, BF16), hm_shape(CHUNK, BF16), hm_shape(1, F32),
                   jax.ShapeDtypeStruct((n, RWKV_W), F32), jax.ShapeDtypeStruct((n, RWKV_W), F32)],
        scratch_shapes=[pltpu.VMEM((tt + SUBLANES, RWKV_IN), F32)],
        compiler_params=_params(),
    )(pr, pr, *consts)


def _rwkv_core_kernel(ar_ref, bk_ref, v_ref, wc_ref, y_ref, ht_ref):
    @pl.when(pl.program_id(1) == 0)
    def _():
        ht_ref[...] = jnp.zeros_like(ht_ref)

    two = 2 * CHUNK
    row = lax.broadcasted_iota(jnp.int32, (two, two), 0)
    col = lax.broadcasted_iota(jnp.int32, (two, two), 1)
    rm = row % CHUNK
    cm = col % CHUNK
    mask = (cm < rm) | ((row >= CHUNK) & (cm == rm))
    eye = (lax.broadcasted_iota(jnp.int32, (CHUNK, CHUNK), 0)
           == lax.broadcasted_iota(jnp.int32, (CHUNK, CHUNK), 1)).astype(F32)
    zeros_v = jnp.zeros((CHUNK, RWKV_HEAD_DIM), BF16)
    dot = functools.partial(jnp.dot, preferred_element_type=F32)

    pre = {}
    for h in range(RWKV_HEADS):
        for g in range(CORE_CHUNKS):
            ar = ar_ref[0, h, g]
            bk = bk_ref[0, h, g]
            vv = v_ref[0, h, g]
            s = jnp.where(mask, lax.dot_general(ar, bk, NT_DIMS, preferred_element_type=F32), 0.0)
            a_ab = s[0:CHUNK, 0:CHUNK]
            p = eye + a_ab
            qb = a_ab.astype(BF16)
            q = dot(qb, qb)
            for _ in range(4):
                qb = q.astype(BF16)
                both = dot(jnp.concatenate([qb, p.astype(BF16)], axis=0), qb)
                q = both[0:CHUNK]
                p = p + both[CHUNK:]
            p = p + dot(p.astype(BF16), q.astype(BF16))
            sb = s.astype(BF16)
            av = dot(sb[0:CHUNK], jnp.concatenate([zeros_v, vv], axis=0))
            pre[h, g] = (ar, bk, vv, sb, p.astype(BF16), av)

    for g in range(CORE_CHUNKS):
        ys = []
        for h in range(RWKV_HEADS):
            ar, bk, vv, sb, tb, av = pre[h, g]
            ht = ht_ref[h]
            arh = lax.dot_general(ar, ht.astype(BF16), NT_DIMS, preferred_element_type=F32)
            u = dot(tb, (arh[0:CHUNK] + av).astype(BF16))
            uv = jnp.concatenate([u.astype(BF16), vv], axis=0)
            ys.append(arh[CHUNK:] + dot(sb[CHUNK:], uv))
            upd = lax.dot_general(uv, bk, TN_DIMS, preferred_element_type=F32)
            ht_ref[h] = wc_ref[0, h, g] * (ht + upd)
        y_ref[g * CHUNK:(g + 1) * CHUNK, :] = jnp.concatenate(ys, axis=1)


def _rwkv_core(ar, bk, v, wc):
    bsz, nh, nc = ar.shape[:3]
    g = CORE_CHUNKS
    steps = nc // g
    hm = lambda rows: pl.BlockSpec((1, nh, g, rows, RWKV_HEAD_DIM), lambda b, i: (b, 0, i, 0, 0))
    return pl.pallas_call(
        _rwkv_core_kernel,
        grid=(bsz, steps),
        in_specs=[hm(2 * CHUNK), hm(2 * CHUNK), hm(CHUNK), hm(1)],
        out_specs=pl.BlockSpec((g * CHUNK, RWKV_W), lambda b, i: (b * steps + i, 0)),
        out_shape=jax.ShapeDtypeStruct((bsz * nc * CHUNK, RWKV_W), F32),
        scratch_shapes=[pltpu.VMEM((nh, RWKV_HEAD_DIM, RWKV_HEAD_DIM), F32)],
        compiler_params=_params(2),
    )(ar, bk, v, wc)


def _mixout_kernel(x_ref, c_ref, y_ref, gate_ref, bonus_ref, lng_ref, lnb_ref, ones_ref, wc_ref, wy_ref, o_ref):
    inv_n = 1.0 / RWKV_HEAD_DIM
    y = y_ref[...]
    hi, mid, _ = _split3(y)
    mu = (jnp.dot(hi, ones_ref[...], preferred_element_type=F32)
          + jnp.dot(mid, ones_ref[...], preferred_element_type=F32)) * inv_n
    yc = y - mu
    var = jnp.dot((yc * yc).astype(BF16), ones_ref[...], preferred_element_type=F32) * inv_n
    yn = yc * lax.rsqrt(var + RWKV_GN_EPS) * lng_ref[...] + lnb_ref[...]
    yo = ((yn + bonus_ref[...]) * gate_ref[...]).astype(BF16)
    mix = (jnp.dot(c_ref[...], wc_ref[...], preferred_element_type=F32)
           + jnp.dot(yo, wy_ref[...], preferred_element_type=F32))
    o_ref[...] = x_ref[...] + mix


def _mixout(x2, c, y, gate, bonus, lng, lnb, ones, w_c, w_y, tt):
    n, d = x2.shape
    row = lambda w: pl.BlockSpec((tt, w), lambda i: (i, 0))
    consts = [lng, lnb, ones, w_c, w_y]
    return pl.pallas_call(
        _mixout_kernel,
        grid=(n // tt,),
        in_specs=[row(d), row(CONV_W), row(RWKV_W), row(RWKV_W), row(RWKV_W)] + [_const_spec(c_.shape) for c_ in consts],
        out_specs=row(d),
        out_shape=jax.ShapeDtypeStruct((n, d), F32),
        compiler_params=_params(),
    )(x2, c, y, gate, bonus, *consts)


def _ffn_kernel(x_ref, halo_ref, g_ref, wup_ref, cw_ref, cb_ref, wdn_ref, o_ref, gs_ref, *, tiles_per_seq):
    tt = x_ref.shape[0]
    keep = (pl.program_id(0) % tiles_per_seq != 0).astype(F32)
    x = x_ref[...]
    xe = jnp.concatenate([halo_ref[...] * keep, x], axis=0)
    h = _rms_norm(xe, g_ref[...]).astype(BF16)
    u = jnp.dot(h, wup_ref[...], preferred_element_type=F32)
    gs_ref[...] = u[:, :D_FF]
    gate = cb_ref[...]
    for j in range(FFN_CONV_K):
        gate = gate + cw_ref[j:j + 1, :] * gs_ref[pl.ds(SUBLANES - (FFN_CONV_K - 1) + j, tt), :]
    act = (gate * _sigmoid(gate) * u[SUBLANES:, D_FF:]).astype(BF16)
    o_ref[...] = x + jnp.dot(act, wdn_ref[...], preferred_element_type=F32)


def _ffn(x2, g, wup, cw, cb, wdn, seq, tt):
    n, d = x2.shape
    tps = seq // tt
    hb = tt // SUBLANES
    kern = functools.partial(_ffn_kernel, tiles_per_seq=tps)
    consts = [g, wup, cw, cb, wdn]
    return pl.pallas_call(
        kern,
        grid=(n // tt,),
        in_specs=[pl.BlockSpec((tt, d), lambda i: (i, 0)),
                  pl.BlockSpec((SUBLANES, d), lambda i: (jnp.maximum(i * hb - 1, 0), 0))]
                 + [_const_spec(c.shape) for c in consts],
        out_specs=pl.BlockSpec((tt, d), lambda i: (i, 0)),
        out_shape=jax.ShapeDtypeStruct((n, d), F32),
        scratch_shapes=[pltpu.VMEM((tt + SUBLANES, D_FF), F32)],
        compiler_params=_params(),
    )(x2, x2, *consts)


def _qkv_kernel(x_ref, pos_ref, g_ref, w_ref, b_ref, qg_ref, kg_ref, freq_ref, sina_ref, sinb_ref, ones_ref,
                q_ref, k_ref, v_ref):
    qd = N_HEADS * HEAD_DIM
    kd = N_KV_HEADS * HEAD_DIM
    h = _rms_norm(x_ref[...], g_ref[...]).astype(BF16)
    qkv = jnp.dot(h, w_ref[...], preferred_element_type=F32) + b_ref[...]

    ang = pos_ref[...] * freq_ref[...]
    cos = jnp.cos(ang)
    sin = jnp.sin(ang)
    sin_a = sin * sina_ref[...]
    sin_b = sin * sinb_ref[...]
    half = ROT_DIM // 2
    lanes = cos.shape[1]

    def norm_rope(z, gain, scale):
        ss = jnp.dot((z * z).astype(BF16), ones_ref[...], preferred_element_type=F32)
        z = z * lax.rsqrt(ss * (1.0 / HEAD_DIM) + NORM_EPS) * gain
        outs = []
        for c0 in range(0, z.shape[1], lanes):
            zz = z[:, c0:c0 + lanes]
            rot = (zz * cos + pltpu.roll(zz, lanes - half, 1) * sin_a + pltpu.roll(zz, half, 1) * sin_b)
            outs.append(rot * scale)
        return outs

    blk = 4 * HEAD_DIM
    for c0 in range(0, qd, blk):
        outs = norm_rope(qkv[:, c0:c0 + blk], qg_ref[...], HEAD_DIM ** -0.5)
        for j, o in enumerate(outs):
            hh = c0 // HEAD_DIM + 2 * j
            q_ref[0, hh] = o[:, :HEAD_DIM].astype(BF16)
            q_ref[0, hh + 1] = o[:, HEAD_DIM:].astype(BF16)
    outs = norm_rope(qkv[:, qd:qd + kd], kg_ref[...], 1.0)
    for j, o in enumerate(outs):
        k_ref[0, 2 * j] = o[:, :HEAD_DIM].astype(BF16)
        k_ref[0, 2 * j + 1] = o[:, HEAD_DIM:].astype(BF16)
    vv = qkv[:, qd + kd:]
    for j in range(N_KV_HEADS):
        v_ref[0, j] = vv[:, j * HEAD_DIM:(j + 1) * HEAD_DIM].astype(BF16)


def _qkv(x2, pos, g, w, b, qg, kg, freq, sina, sinb, ones, bsz, seq, tt):
    n, d = x2.shape
    tps = seq // tt
    consts = [g, w, b, qg, kg, freq, sina, sinb, ones]
    hm = lambda nh: pl.BlockSpec((1, nh, tt, HEAD_DIM), lambda i: (i // tps, 0, i % tps, 0))
    return pl.pallas_call(
        _qkv_kernel,
        grid=(n // tt,),
        in_specs=[pl.BlockSpec((tt, d), lambda i: (i, 0)), pl.BlockSpec((tt, 1), lambda i: (i, 0))]
                 + [_const_spec(c.shape) for c in consts],
        out_specs=[hm(N_HEADS), hm(N_KV_HEADS), hm(N_KV_HEADS)],
        out_shape=[jax.ShapeDtypeStruct((bsz, N_HEADS, seq, HEAD_DIM), BF16),
                   jax.ShapeDtypeStruct((bsz, N_KV_HEADS, seq, HEAD_DIM), BF16),
                   jax.ShapeDtypeStruct((bsz, N_KV_HEADS, seq, HEAD_DIM), BF16)],
        compiler_params=_params(),
    )(x2, pos, *consts)


def _attn_kernel(sink_ref, q_ref, kc_ref, kp_ref, vc_ref, vp_ref, o_ref, *, q_blocks):
    i = pl.program_id(1)
    blk = WINDOW
    rows = GROUP * blk
    t = lax.broadcasted_iota(jnp.int32, (rows, 2 * blk), 0) % blk
    c = lax.broadcasted_iota(jnp.int32, (rows, 2 * blk), 1)
    d = c - t
    band = (d >= 1) & (d <= blk)
    head_of_row = lax.broadcasted_iota(jnp.int32, (rows, 1), 0) // blk
    for qb in range(q_blocks):
        if qb == 0:
            valid = band & ((c >= blk) | (i > 0))
        else:
            valid = band
        outs = []
        for g in range(N_KV_HEADS):
            q = q_ref[0, g * GROUP:(g + 1) * GROUP, qb * blk:(qb + 1) * blk, :].reshape(rows, HEAD_DIM)
            if qb == 0:
                kprev, vprev = kp_ref[0, g], vp_ref[0, g]
            else:
                kprev = kc_ref[0, g, (qb - 1) * blk:qb * blk, :]
                vprev = vc_ref[0, g, (qb - 1) * blk:qb * blk, :]
            k = jnp.concatenate([kprev, kc_ref[0, g, qb * blk:(qb + 1) * blk, :]], axis=0)
            v = jnp.concatenate([vprev, vc_ref[0, g, qb * blk:(qb + 1) * blk, :]], axis=0)
            s = lax.dot_general(q, k, NT_DIMS, preferred_element_type=F32)
            s = jnp.where(valid, s, -jnp.inf)
            sink = jnp.zeros((rows, 1), F32)
            for j in range(GROUP):
                sink = jnp.where(head_of_row == j, sink_ref[g * GROUP + j], sink)
            m = jnp.maximum(jnp.max(s, axis=-1, keepdims=True), sink)
            p = jnp.exp(s - m)
            den = jnp.sum(p, axis=-1, keepdims=True) + jnp.exp(sink - m)
            o = jnp.dot(p.astype(BF16), v, preferred_element_type=F32) / den
            outs.extend(o[j * blk:(j + 1) * blk] for j in range(GROUP))
        o_ref[qb * blk:(qb + 1) * blk, :] = jnp.concatenate(outs, axis=1).astype(o_ref.dtype)


def _attention(sinks, q, k, v, q_blocks):
    bsz, _, seq, _ = q.shape
    tq = q_blocks * WINDOW
    steps = seq // tq
    kern = functools.partial(_attn_kernel, q_blocks=q_blocks)
    cur = lambda nh: pl.BlockSpec((1, nh, tq, HEAD_DIM), lambda b, i: (b, 0, i, 0))
    prev = pl.BlockSpec((1, N_KV_HEADS, WINDOW, HEAD_DIM), lambda b, i: (b, 0, jnp.maximum(i * q_blocks - 1, 0), 0))
    return pl.pallas_call(
        kern,
        grid=(bsz, steps),
        in_specs=[pl.BlockSpec(memory_space=pltpu.SMEM), cur(N_HEADS), cur(N_KV_HEADS), prev, cur(N_KV_HEADS), prev],
        out_specs=pl.BlockSpec((tq, N_HEADS * HEAD_DIM), lambda b, i: (b * steps + i, 0)),
        out_shape=jax.ShapeDtypeStruct((bsz * seq, N_HEADS * HEAD_DIM), BF16),
        compiler_params=_params(2),
    )(sinks, q, k, k, v, v)


def _attn_out_kernel(x_ref, o_ref, w_ref, b_ref, out_ref):
    out_ref[...] = x_ref[...] + jnp.dot(o_ref[...], w_ref[...], preferred_element_type=F32) + b_ref[...]


def _attn_out(x2, o, w, b, tt):
    n, d = x2.shape
    return pl.pallas_call(
        _attn_out_kernel,
        grid=(n // tt,),
        in_specs=[pl.BlockSpec((tt, d), lambda i: (i, 0)), pl.BlockSpec((tt, o.shape[1]), lambda i: (i, 0)),
                  _const_spec(w.shape), _const_spec(b.shape)],
        out_specs=pl.BlockSpec((tt, d), lambda i: (i, 0)),
        out_shape=jax.ShapeDtypeStruct((n, d), F32),
        compiler_params=_params(),
    )(x2, o, w, b)


def _block_ones(width, group):
    idx = jnp.arange(width) // group
    return (idx[:, None] == idx[None, :]).astype(BF16)


def _chunk_tril(rows):
    idx = jnp.arange(rows)
    same = (idx[:, None] // CHUNK) == (idx[None, :] // CHUNK)
    return (same & (idx[None, :] <= idx[:, None])).astype(BF16)


def _rope_lane_tables(lanes=128):
    half = ROT_DIM // 2
    inv_freq = ROPE_THETA ** (-(jnp.arange(half, dtype=F32) * 2.0) / ROT_DIM)
    dim = jnp.arange(lanes) % HEAD_DIM
    freq = jnp.where(dim < ROT_DIM, inv_freq[dim % half], 0.0).astype(F32)
    sin_a = jnp.where(dim < half, -1.0, 0.0).astype(F32)
    sin_b = jnp.where((dim >= half) & (dim < ROT_DIM), 1.0, 0.0).astype(F32)
    return freq[None, :], sin_a[None, :], sin_b[None, :]


def _row(v):
    return v.reshape(1, -1)


def kernel(x, positions, ab_norm_g, ab_w_in, conv_in_b, conv_dw_w, conv_dw_b, conv_ln_g, conv_ln_b, rwkv_mu, rwkv_w0, rwkv_w2, rwkv_a0, rwkv_a2, rwkv_g2, rwkv_k_k, rwkv_k_a, rwkv_r_k, rwkv_ln_g, rwkv_ln_b, ab_w_out, attn_norm_g, attn_w_qkv, attn_b_qkv, attn_q_norm_g, attn_k_norm_g, attn_sinks, attn_w_o, attn_b_o, ffn_norm_g, ffn_w_up, ffn_conv_w, ffn_conv_b, ffn_w_down):
    bsz, seq, d = x.shape
    depth = ffn_norm_g.shape[0]
    n = bsz * seq
    tt = 512
    ffn_tt = 256
    x2 = x.reshape(n, d)
    ones_rwkv = _block_ones(RWKV_W, RWKV_HEAD_DIM)
    ones_attn = _block_ones(4 * HEAD_DIM, HEAD_DIM)
    tri = _chunk_tril(256)
    freq, sin_a, sin_b = _rope_lane_tables()
    pos = positions.reshape(n, 1).astype(F32)

    for layer in range(depth):
        i = layer // 2
        if layer % 2 == 0:
            pc, pr = _inproj(x2, _row(ab_norm_g[i]), ab_w_in[i].astype(BF16), tt)
            c = _convmod(pc, _row(conv_in_b[i]), conv_dw_w[i], _row(conv_dw_b[i]), _row(conv_ln_g[i]),
                         _row(conv_ln_b[i]), seq, tt)
            zeros = jnp.zeros((LORA_DECAY, RWKV_W), F32)
            w2a2 = jnp.concatenate([jnp.concatenate([rwkv_w2[i], zeros], axis=1),
                                    jnp.concatenate([zeros, rwkv_a2[i]], axis=1)], axis=0).astype(BF16)
            ar, bk, vv, wc, gate, bonus = _rwkv_prep(
                pr, _row(rwkv_mu[i]), _row(rwkv_w0[i]), w2a2, _row(rwkv_a0[i]), rwkv_g2[i].astype(BF16),
                _row(rwkv_k_k[i]), _row(rwkv_k_a[i]), _row(rwkv_r_k[i]), ones_rwkv, tri, bsz, seq, tt)
            y = _rwkv_core(ar, bk, vv, wc)
            w_out = ab_w_out[i].astype(BF16)
            x2 = _mixout(x2, c, y, gate, bonus, _row(rwkv_ln_g[i]), _row(rwkv_ln_b[i]), ones_rwkv,
                         w_out[:CONV_W], w_out[CONV_W:], tt)
        else:
            q, k, v = _qkv(x2, pos, _row(attn_norm_g[i]), attn_w_qkv[i].astype(BF16), _row(attn_b_qkv[i]),
                           _row(jnp.tile(attn_q_norm_g[i], 4)), _row(jnp.tile(attn_k_norm_g[i], 4)),
                           freq, sin_a, sin_b, ones_attn, bsz, seq, tt)
            o = _attention(attn_sinks[i], q, k, v, 2)
            x2 = _attn_out(x2, o, attn_w_o[i].astype(BF16), _row(attn_b_o[i]), tt)
        x2 = _ffn(x2, _row(ffn_norm_g[layer]), ffn_w_up[layer].astype(BF16), ffn_conv_w[layer],
                  _row(ffn_conv_b[layer]), ffn_w_down[layer].astype(BF16), seq, ffn_tt)
    return x2.reshape(bsz, seq, d)
```

```python
import functools

import jax
import jax.numpy as jnp
import numpy as np
from jax import lax
from jax.experimental import pallas as pl
from jax.experimental.pallas import tpu as pltpu

F32 = jnp.float32
BF16 = jnp.bfloat16

CONV_W = 512
CONV_K = 31
CONV_LN_EPS = 1e-5
RWKV_HEADS = 8
RWKV_HEAD_DIM = 64
RWKV_W = RWKV_HEADS * RWKV_HEAD_DIM
LORA_DECAY = 64
LORA_ICLR = 64
LORA_GATE = 128
RWKV_GN_EPS = RWKV_HEAD_DIM * 1e-5
RWKV_IN = 3 * RWKV_W + LORA_DECAY + LORA_ICLR + LORA_GATE
HEAD_DIM = 64
N_HEADS = 16
N_KV_HEADS = 4
GROUP = N_HEADS // N_KV_HEADS
WINDOW = 128
ROT_DIM = 16
ROPE_THETA = 500000.0
D_FF = 2816
FFN_CONV_K = 3
NORM_EPS = 1e-6

CHUNK = 64
CORE_CHUNKS = 8
SUBLANES = 8
CONV_HALO = 32
V7X_VMEM_LIMIT = 56 * 1024 * 1024

NT_DIMS = (((1,), (1,)), ((), ()))
TN_DIMS = (((0,), (0,)), ((), ()))


def _params(n_axes=1):
    return pltpu.CompilerParams(dimension_semantics=("arbitrary",) * n_axes,
                                vmem_limit_bytes=V7X_VMEM_LIMIT)


def _const_spec(shape):
    nd = len(shape)
    return pl.BlockSpec(shape, lambda *_: (0,) * nd, pipeline_mode=pl.Buffered(1))


def _rms_norm(x, g):
    return x * lax.rsqrt(jnp.mean(x * x, axis=-1, keepdims=True) + NORM_EPS) * g


def _sigmoid(z):
    return 1.0 / (1.0 + jnp.exp(-z))


def _split3(z):
    hi = z.astype(BF16)
    r1 = z - hi.astype(F32)
    mid = r1.astype(BF16)
    lo = (r1 - mid.astype(F32)).astype(BF16)
    return hi, mid, lo


def _inproj_kernel(x_ref, g_ref, w_ref, pc_ref, pr_ref):
    h = _rms_norm(x_ref[...], g_ref[...])
    p = jnp.dot(h.astype(BF16), w_ref[...], preferred_element_type=F32)
    pc_ref[...] = p[:, :2 * CONV_W]
    pr_ref[...] = p[:, 2 * CONV_W:]


def _inproj(x2, g, w, tt):
    n, d = x2.shape
    return pl.pallas_call(
        _inproj_kernel,
        grid=(n // tt,),
        in_specs=[pl.BlockSpec((tt, d), lambda i: (i, 0)), _const_spec(g.shape), _const_spec(w.shape)],
        out_specs=[pl.BlockSpec((tt, 2 * CONV_W), lambda i: (i, 0)),
                   pl.BlockSpec((tt, RWKV_IN), lambda i: (i, 0))],
        out_shape=[jax.ShapeDtypeStruct((n, 2 * CONV_W), F32), jax.ShapeDtypeStruct((n, RWKV_IN), F32)],
        compiler_params=_params(),
    )(x2, g, w)


def _convmod_kernel(pc_ref, halo_ref, inb_ref, dww_ref, dwb_ref, lng_ref, lnb_ref, c_ref, ext_ref,
                    *, tiles_per_seq, row_block):
    tt = pc_ref.shape[0]

    def glu(z):
        z = z + inb_ref[...]
        return z[:, :CONV_W] * _sigmoid(z[:, CONV_W:])

    keep = (pl.program_id(0) % tiles_per_seq != 0).astype(F32)
    ext_ref[0, 0:CONV_HALO, :] = glu(halo_ref[...]) * keep
    ext_ref[0, CONV_HALO:, :] = glu(pc_ref[...])
    span = tt + CONV_HALO - SUBLANES
    for s in range(1, SUBLANES):
        ext_ref[s, 0:span, :] = ext_ref[0, pl.ds(s, span), :]
    first = CONV_HALO - (CONV_K - 1)
    for r0 in range(0, tt, row_block):
        acc = jnp.broadcast_to(dwb_ref[...], (row_block, CONV_W))
        for s in range(SUBLANES):
            taps = [j for j in range(CONV_K) if (first + j) % SUBLANES == s]
            lo = first + taps[0] - s
            hi = first + taps[-1] - s
            win = ext_ref[s, pl.ds(r0 + lo, row_block + hi - lo), :]
            for j in taps:
                a = first + j - s - lo
                acc = acc + dww_ref[j:j + 1, :] * win[a:a + row_block]
        mu = jnp.mean(acc, axis=-1, keepdims=True)
        xc = acc - mu
        var = jnp.mean(xc * xc, axis=-1, keepdims=True)
        z = xc * lax.rsqrt(var + CONV_LN_EPS) * lng_ref[...] + lnb_ref[...]
        c_ref[r0:r0 + row_block, :] = (z * _sigmoid(z)).astype(c_ref.dtype)


def _convmod(pc, inb, dww, dwb, lng, lnb, seq, tt):
    n = pc.shape[0]
    tps = seq // tt
    hb = tt // CONV_HALO
    kern = functools.partial(_convmod_kernel, tiles_per_seq=tps, row_block=64)
    return pl.pallas_call(
        kern,
        grid=(n // tt,),
        in_specs=[pl.BlockSpec((tt, 2 * CONV_W), lambda i: (i, 0)),
                  pl.BlockSpec((CONV_HALO, 2 * CONV_W), lambda i: (jnp.maximum(i * hb - 1, 0), 0)),
                  _const_spec(inb.shape), _const_spec(dww.shape), _const_spec(dwb.shape),
                  _const_spec(lng.shape), _const_spec(lnb.shape)],
        out_specs=pl.BlockSpec((tt, CONV_W), lambda i: (i, 0)),
        out_shape=jax.ShapeDtypeStruct((n, CONV_W), BF16),
        scratch_shapes=[pltpu.VMEM((SUBLANES, tt + CONV_HALO, CONV_W), F32)],
        compiler_params=_params(),
    )(pc, pc, inb, dww, dwb, lng, lnb)


def _rwkv_prep_kernel(pr_ref, halo_ref, mu_ref, w0_ref, w2a2_ref, a0_ref, g2_ref, kk_ref, ka_ref, rk_ref,
                      ones_ref, tri_ref,
                      ar_ref, bk_ref, v_ref, wc_ref, gate_ref, bonus_ref, ext_ref, *, tiles_per_seq):
    tt = pr_ref.shape[0]
    keep = (pl.program_id(0) % tiles_per_seq != 0).astype(F32)
    ext_ref[0:SUBLANES, :] = halo_ref[...] * keep
    ext_ref[SUBLANES:, :] = pr_ref[...]
    rw = pr_ref[...]
    rw = rw + (ext_ref[pl.ds(SUBLANES - 1, tt), :] - rw) * mu_ref[...]

    r = rw[:, 0:RWKV_W]
    k = rw[:, RWKV_W:2 * RWKV_W]
    v = rw[:, 2 * RWKV_W:3 * RWKV_W]
    wa = rw[:, 3 * RWKV_W:3 * RWKV_W + LORA_DECAY + LORA_ICLR]
    gd = rw[:, 3 * RWKV_W + LORA_DECAY + LORA_ICLR:]

    lane = lax.broadcasted_iota(jnp.int32, wa.shape, 1)
    z = jnp.where(lane < LORA_DECAY, jnp.tanh(wa), wa)
    proj = jnp.dot(z.astype(BF16), w2a2_ref[...], preferred_element_type=F32)
    zw = -(w0_ref[...] + proj[:, :RWKV_W])
    softplus = jnp.maximum(zw, 0.0) + jnp.log(1.0 + jnp.exp(-jnp.abs(zw)))
    lw = -jnp.exp(-softplus - 0.5)
    a = _sigmoid(a0_ref[...] + proj[:, RWKV_W:])
    gate_ref[...] = jnp.dot(_sigmoid(gd).astype(BF16), g2_ref[...], preferred_element_type=F32)

    kk = k * kk_ref[...]
    ss = jnp.dot((kk * kk).astype(BF16), ones_ref[...], preferred_element_type=F32)
    kk = kk / jnp.maximum(jnp.sqrt(ss), 1e-12)
    k = k * (1.0 + (a - 1.0) * ka_ref[...])
    rk = jnp.dot((r * k * rk_ref[...]).astype(BF16), ones_ref[...], preferred_element_type=F32)
    bonus_ref[...] = rk * v

    half = tri_ref.shape[0]
    cums = []
    for r0 in range(0, tt, half):
        parts = _split3(lw[r0:r0 + half])
        cums.append(sum(jnp.dot(tri_ref[...], p, preferred_element_type=F32) for p in parts))
    cum = jnp.concatenate(cums, axis=0)
    e_cum = jnp.exp(cum)
    e_inv = jnp.exp(-cum)
    rt = r * e_cum
    at = -kk * jnp.exp(cum - lw)
    kt = k * e_inv
    bt = kk * a * e_inv

    nch = tt // CHUNK
    for h in range(RWKV_HEADS):
        sl = slice(h * RWKV_HEAD_DIM, (h + 1) * RWKV_HEAD_DIM)
        for c in range(nch):
            rows = slice(c * CHUNK, (c + 1) * CHUNK)
            ar_ref[0, h, c, 0:CHUNK, :] = at[rows, sl].astype(BF16)
            ar_ref[0, h, c, CHUNK:, :] = rt[rows, sl].astype(BF16)
            bk_ref[0, h, c, 0:CHUNK, :] = bt[rows, sl].astype(BF16)
            bk_ref[0, h, c, CHUNK:, :] = kt[rows, sl].astype(BF16)
            v_ref[0, h, c, :, :] = v[rows, sl].astype(BF16)
            wc_ref[0, h, c, :, :] = e_cum[(c + 1) * CHUNK - 1:(c + 1) * CHUNK, sl]


def _rwkv_prep(pr, mu, w0, w2a2, a0, g2, k_k, k_a, r_k, ones, tri, bsz, seq, tt):
    n = pr.shape[0]
    tps = seq // tt
    hb = tt // SUBLANES
    nch = tt // CHUNK
    nc = seq // CHUNK
    kern = functools.partial(_rwkv_prep_kernel, tiles_per_seq=tps)
    hm = lambda rows: pl.BlockSpec((1, RWKV_HEADS, nch, rows, RWKV_HEAD_DIM),
                                   lambda i: (i // tps, 0, i % tps, 0, 0))
    hm_shape = lambda rows, dt: jax.ShapeDtypeStruct((bsz, RWKV_HEADS, nc, rows, RWKV_HEAD_DIM), dt)
    consts = [mu, w0, w2a2, a0, g2, k_k, k_a, r_k, ones, tri]
    return pl.pallas_call(
        kern,
        grid=(n // tt,),
        in_specs=[pl.BlockSpec((tt, RWKV_IN), lambda i: (i, 0)),
                  pl.BlockSpec((SUBLANES, RWKV_IN), lambda i: (jnp.maximum(i * hb - 1, 0), 0))]
                 + [_const_spec(c.shape) for c in consts],
        out_specs=[hm(2 * CHUNK), hm(2 * CHUNK), hm(CHUNK), hm(1),
                   pl.BlockSpec((tt, RWKV_W), lambda i: (i, 0)),
                   pl.BlockSpec((tt, RWKV_W), lambda i: (i, 0))],
        out_shape=[hm_shape(2 * CHUNK, BF16), hm_shape(2 * CHUNK, BF16), hm_shape(CHUNK, BF16), hm_shape(1, F32),
                   jax.ShapeDtypeStruct((n, RWKV_W), F32), jax.ShapeDtypeStruct((n, RWKV_W), F32)],
        scratch_shapes=[pltpu.VMEM((tt + SUBLANES, RWKV_IN), F32)],
        compiler_params=_params(),
    )(pr, pr, *consts)


def _rwkv_core_kernel(ar_ref, bk_ref, v_ref, wc_ref, y_ref, ht_ref):
    @pl.when(pl.program_id(1) == 0)
    def _():
        ht_ref[...] = jnp.zeros_like(ht_ref)

    two = 2 * CHUNK
    row = lax.broadcasted_iota(jnp.int32, (two, two), 0)
    col = lax.broadcasted_iota(jnp.int32, (two, two), 1)
    rm = row % CHUNK
    cm = col % CHUNK
    mask = (cm < rm) | ((row >= CHUNK) & (cm == rm))
    eye = (lax.broadcasted_iota(jnp.int32, (CHUNK, CHUNK), 0)
           == lax.broadcasted_iota(jnp.int32, (CHUNK, CHUNK), 1)).astype(F32)
    bmm = functools.partial(jnp.einsum, preferred_element_type=F32)
    nb = RWKV_HEADS * CORE_CHUNKS
    ar = ar_ref[0].reshape(nb, two, RWKV_HEAD_DIM)
    bk = bk_ref[0].reshape(nb, two, RWKV_HEAD_DIM)
    vv = v_ref[0].reshape(nb, CHUNK, RWKV_HEAD_DIM)
    s = jnp.where(mask[None], bmm('bmk,bnk->bmn', ar, bk), 0.0)
    a_ab = s[:, 0:CHUNK, 0:CHUNK]
    p = eye[None] + a_ab
    qb = a_ab.astype(BF16)
    q = bmm('bij,bjk->bik', qb, qb)
    for _ in range(4):
        qb = q.astype(BF16)
        both = bmm('bij,bjk->bik', jnp.concatenate([qb, p.astype(BF16)], axis=1), qb)
        q = both[:, 0:CHUNK]
        p = p + both[:, CHUNK:]
    p = p + bmm('bij,bjk->bik', p.astype(BF16), q.astype(BF16))
    sb = s.astype(BF16)
    av = bmm('bij,bjk->bik', sb[:, 0:CHUNK], jnp.concatenate([jnp.zeros_like(vv), vv], axis=1))

    def per_chunk(z, g):
        return z.reshape((RWKV_HEADS, CORE_CHUNKS) + z.shape[1:])[:, g]

    tb = p.astype(BF16)
    for g in range(CORE_CHUNKS):
        ht = ht_ref[...]
        bk_g = per_chunk(bk, g)
        vv_g = per_chunk(vv, g)
        sb_g = per_chunk(sb, g)
        arh = bmm('hmk,hnk->hmn', per_chunk(ar, g), ht.astype(BF16))
        u = bmm('hij,hjk->hik', per_chunk(tb, g), (arh[:, 0:CHUNK] + per_chunk(av, g)).astype(BF16))
        uv = jnp.concatenate([u.astype(BF16), vv_g], axis=1)
        y = arh[:, CHUNK:] + bmm('hij,hjk->hik', sb_g[:, CHUNK:], uv)
        upd = bmm('hsi,hsj->hij', uv, bk_g)
        ht_ref[...] = wc_ref[0, :, g] * (ht + upd)
        y_ref[g * CHUNK:(g + 1) * CHUNK, :] = jnp.concatenate([y[h] for h in range(RWKV_HEADS)], axis=1)


def _rwkv_core(ar, bk, v, wc):
    bsz, nh, nc = ar.shape[:3]
    g = CORE_CHUNKS
    steps = nc // g
    hm = lambda rows: pl.BlockSpec((1, nh, g, rows, RWKV_HEAD_DIM), lambda b, i: (b, 0, i, 0, 0))
    return pl.pallas_call(
        _rwkv_core_kernel,
        grid=(bsz, steps),
        in_specs=[hm(2 * CHUNK), hm(2 * CHUNK), hm(CHUNK), hm(1)],
        out_specs=pl.BlockSpec((g * CHUNK, RWKV_W), lambda b, i: (b * steps + i, 0)),
        out_shape=jax.ShapeDtypeStruct((bsz * nc * CHUNK, RWKV_W), F32),
        scratch_shapes=[pltpu.VMEM((nh, RWKV_HEAD_DIM, RWKV_HEAD_DIM), F32)],
        compiler_params=_params(2),
    )(ar, bk, v, wc)


def _mixout_kernel(x_ref, c_ref, y_ref, gate_ref, bonus_ref, lng_ref, lnb_ref, ones_ref, w_ref, o_ref):
    inv_n = 1.0 / RWKV_HEAD_DIM
    y = y_ref[...]
    hi, mid, _ = _split3(y)
    mu = (jnp.dot(hi, ones_ref[...], preferred_element_type=F32)
          + jnp.dot(mid, ones_ref[...], preferred_element_type=F32)) * inv_n
    yc = y - mu
    var = jnp.dot((yc * yc).astype(BF16), ones_ref[...], preferred_element_type=F32) * inv_n
    yn = yc * lax.rsqrt(var + RWKV_GN_EPS) * lng_ref[...] + lnb_ref[...]
    yo = ((yn + bonus_ref[...]) * gate_ref[...]).astype(BF16)
    mix = (jnp.dot(c_ref[...], w_ref[0:CONV_W, :], preferred_element_type=F32)
           + jnp.dot(yo, w_ref[CONV_W:, :], preferred_element_type=F32))
    o_ref[...] = x_ref[...] + mix


def _mixout(x2, c, y, gate, bonus, lng, lnb, ones, w_out, tt):
    n, d = x2.shape
    row = lambda w: pl.BlockSpec((tt, w), lambda i: (i, 0))
    consts = [lng, lnb, ones, w_out]
    return pl.pallas_call(
        _mixout_kernel,
        grid=(n // tt,),
        in_specs=[row(d), row(CONV_W), row(RWKV_W), row(RWKV_W), row(RWKV_W)] + [_const_spec(c_.shape) for c_ in consts],
        out_specs=row(d),
        out_shape=jax.ShapeDtypeStruct((n, d), F32),
        compiler_params=_params(),
    )(x2, c, y, gate, bonus, *consts)


def _ffn_kernel(x_ref, halo_ref, g_ref, wup_ref, cw_ref, cb_ref, wdn_ref, o_ref, gs_ref, *, tiles_per_seq):
    tt = x_ref.shape[0]
    keep = (pl.program_id(0) % tiles_per_seq != 0).astype(F32)
    x = x_ref[...]
    xe = jnp.concatenate([halo_ref[...] * keep, x], axis=0)
    h = _rms_norm(xe, g_ref[...]).astype(BF16)
    u = jnp.dot(h, wup_ref[...], preferred_element_type=F32)
    gs_ref[...] = u[:, :D_FF]
    gate = cb_ref[...]
    for j in range(FFN_CONV_K):
        gate = gate + cw_ref[j:j + 1, :] * gs_ref[pl.ds(SUBLANES - (FFN_CONV_K - 1) + j, tt), :]
    act = (gate * _sigmoid(gate) * u[SUBLANES:, D_FF:]).astype(BF16)
    o_ref[...] = x + jnp.dot(act, wdn_ref[...], preferred_element_type=F32)


def _ffn(x2, g, wup, cw, cb, wdn, seq, tt):
    n, d = x2.shape
    tps = seq // tt
    hb = tt // SUBLANES
    kern = functools.partial(_ffn_kernel, tiles_per_seq=tps)
    consts = [g, wup, cw, cb, wdn]
    return pl.pallas_call(
        kern,
        grid=(n // tt,),
        in_specs=[pl.BlockSpec((tt, d), lambda i: (i, 0)),
                  pl.BlockSpec((SUBLANES, d), lambda i: (jnp.maximum(i * hb - 1, 0), 0))]
                 + [_const_spec(c.shape) for c in consts],
        out_specs=pl.BlockSpec((tt, d), lambda i: (i, 0)),
        out_shape=jax.ShapeDtypeStruct((n, d), F32),
        scratch_shapes=[pltpu.VMEM((tt + SUBLANES, D_FF), F32)],
        compiler_params=_params(),
    )(x2, x2, *consts)


def _rope_table_kernel(freq_ref, pos_ref, cs_ref):
    half = ROT_DIM // 2
    pos = pos_ref[...]
    for f in range(half):
        ang = pos * freq_ref[f]
        cs_ref[f] = jnp.cos(ang)
        cs_ref[half + f] = jnp.sin(ang)


def _rope_table(inv_freq, pos_dense):
    rows, lanes = pos_dense.shape
    return pl.pallas_call(
        _rope_table_kernel,
        in_specs=[pl.BlockSpec(memory_space=pltpu.SMEM), pl.BlockSpec((rows, lanes), lambda: (0, 0))],
        out_specs=pl.BlockSpec((ROT_DIM, rows, lanes), lambda: (0, 0, 0)),
        out_shape=jax.ShapeDtypeStruct((ROT_DIM, rows, lanes), F32),
    )(inv_freq, pos_dense)


def _qkv_kernel(x_ref, cs_ref, g_ref, w_ref, b_ref, qg_ref, kg_ref, expand_ref, ones_ref,
                q_ref, k_ref, v_ref):
    qd = N_HEADS * HEAD_DIM
    kd = N_KV_HEADS * HEAD_DIM
    h = _rms_norm(x_ref[...], g_ref[...]).astype(BF16)
    qkv = jnp.dot(h, w_ref[...], preferred_element_type=F32) + b_ref[...]

    lanes = expand_ref.shape[1] // 3
    cs_hi, cs_lo, _ = _split3(cs_ref[...])
    tab = (jnp.dot(cs_hi, expand_ref[...], preferred_element_type=F32)
           + jnp.dot(cs_lo, expand_ref[...], preferred_element_type=F32))
    dim = lax.broadcasted_iota(jnp.int32, (1, lanes), 1) % HEAD_DIM
    cos = tab[:, :lanes] + (dim >= ROT_DIM).astype(F32)
    sin_a = tab[:, lanes:2 * lanes]
    sin_b = tab[:, 2 * lanes:]
    half = ROT_DIM // 2

    def norm_rope(z, gain, scale):
        ss = jnp.dot((z * z).astype(BF16), ones_ref[...], preferred_element_type=F32)
        z = z * lax.rsqrt(ss * (1.0 / HEAD_DIM) + NORM_EPS) * gain
        outs = []
        for c0 in range(0, z.shape[1], lanes):
            zz = z[:, c0:c0 + lanes]
            rot = (zz * cos + pltpu.roll(zz, lanes - half, 1) * sin_a + pltpu.roll(zz, half, 1) * sin_b)
            outs.append(rot * scale)
        return outs

    blk = 4 * HEAD_DIM
    for c0 in range(0, qd, blk):
        outs = norm_rope(qkv[:, c0:c0 + blk], qg_ref[...], HEAD_DIM ** -0.5)
        for j, o in enumerate(outs):
            hh = c0 // HEAD_DIM + 2 * j
            q_ref[0, hh] = o[:, :HEAD_DIM].astype(BF16)
            q_ref[0, hh + 1] = o[:, HEAD_DIM:].astype(BF16)
    outs = norm_rope(qkv[:, qd:qd + kd], kg_ref[...], 1.0)
    for j, o in enumerate(outs):
        k_ref[0, 2 * j] = o[:, :HEAD_DIM].astype(BF16)
        k_ref[0, 2 * j + 1] = o[:, HEAD_DIM:].astype(BF16)
    vv = qkv[:, qd + kd:]
    for j in range(N_KV_HEADS):
        v_ref[0, j] = vv[:, j * HEAD_DIM:(j + 1) * HEAD_DIM].astype(BF16)


def _qkv(x2, cs, g, w, b, qg, kg, expand, ones, bsz, seq, tt):
    n, d = x2.shape
    tps = seq // tt
    consts = [g, w, b, qg, kg, expand, ones]
    hm = lambda nh: pl.BlockSpec((1, nh, tt, HEAD_DIM), lambda i: (i // tps, 0, i % tps, 0))
    return pl.pallas_call(
        _qkv_kernel,
        grid=(n // tt,),
        in_specs=[pl.BlockSpec((tt, d), lambda i: (i, 0)), pl.BlockSpec((tt, ROT_DIM), lambda i: (i, 0))]
                 + [_const_spec(c.shape) for c in consts],
        out_specs=[hm(N_HEADS), hm(N_KV_HEADS), hm(N_KV_HEADS)],
        out_shape=[jax.ShapeDtypeStruct((bsz, N_HEADS, seq, HEAD_DIM), BF16),
                   jax.ShapeDtypeStruct((bsz, N_KV_HEADS, seq, HEAD_DIM), BF16),
                   jax.ShapeDtypeStruct((bsz, N_KV_HEADS, seq, HEAD_DIM), BF16)],
        compiler_params=_params(),
    )(x2, cs, *consts)


def _attn_kernel(sink_ref, q_ref, kc_ref, kp_ref, vc_ref, vp_ref, x_ref, w_ref, b_ref, out_ref, *, q_blocks):
    i = pl.program_id(1)
    blk = WINDOW
    rows = GROUP * blk
    t = lax.broadcasted_iota(jnp.int32, (rows, 2 * blk), 0) % blk
    c = lax.broadcasted_iota(jnp.int32, (rows, 2 * blk), 1)
    d = c - t
    band = (d >= 1) & (d <= blk)
    head_of_row = lax.broadcasted_iota(jnp.int32, (rows, 1), 0) // blk
    blocks = []
    for qb in range(q_blocks):
        if qb == 0:
            valid = band & ((c >= blk) | (i > 0))
        else:
            valid = band
        outs = []
        for g in range(N_KV_HEADS):
            q = q_ref[0, g * GROUP:(g + 1) * GROUP, qb * blk:(qb + 1) * blk, :].reshape(rows, HEAD_DIM)
            if qb == 0:
                kprev, vprev = kp_ref[0, g], vp_ref[0, g]
            else:
                kprev = kc_ref[0, g, (qb - 1) * blk:qb * blk, :]
                vprev = vc_ref[0, g, (qb - 1) * blk:qb * blk, :]
            k = jnp.concatenate([kprev, kc_ref[0, g, qb * blk:(qb + 1) * blk, :]], axis=0)
            v = jnp.concatenate([vprev, vc_ref[0, g, qb * blk:(qb + 1) * blk, :]], axis=0)
            s = lax.dot_general(q, k, NT_DIMS, preferred_element_type=F32)
            s = jnp.where(valid, s, -jnp.inf)
            sink = jnp.zeros((rows, 1), F32)
            for j in range(GROUP):
                sink = jnp.where(head_of_row == j, sink_ref[g * GROUP + j], sink)
            m = jnp.maximum(jnp.max(s, axis=-1, keepdims=True), sink)
            p = jnp.exp(s - m)
            den = jnp.sum(p, axis=-1, keepdims=True) + jnp.exp(sink - m)
            o = jnp.dot(p.astype(BF16), v, preferred_element_type=F32) / den
            outs.extend(o[j * blk:(j + 1) * blk] for j in range(GROUP))
        blocks.append(jnp.concatenate(outs, axis=1).astype(BF16))
    o_all = jnp.concatenate(blocks, axis=0)
    out_ref[...] = x_ref[...] + jnp.dot(o_all, w_ref[...], preferred_element_type=F32) + b_ref[...]


def _attention(sinks, q, k, v, x2, w_o, b_o, q_blocks):
    bsz, _, seq, _ = q.shape
    d = x2.shape[1]
    tq = q_blocks * WINDOW
    steps = seq // tq
    kern = functools.partial(_attn_kernel, q_blocks=q_blocks)
    cur = lambda nh: pl.BlockSpec((1, nh, tq, HEAD_DIM), lambda b, i: (b, 0, i, 0))
    prev = pl.BlockSpec((1, N_KV_HEADS, WINDOW, HEAD_DIM), lambda b, i: (b, 0, jnp.maximum(i * q_blocks - 1, 0), 0))
    rows = pl.BlockSpec((tq, d), lambda b, i: (b * steps + i, 0))
    return pl.pallas_call(
        kern,
        grid=(bsz, steps),
        in_specs=[pl.BlockSpec(memory_space=pltpu.SMEM), cur(N_HEADS), cur(N_KV_HEADS), prev, cur(N_KV_HEADS), prev,
                  rows, _const_spec(w_o.shape), _const_spec(b_o.shape)],
        out_specs=rows,
        out_shape=jax.ShapeDtypeStruct((bsz * seq, d), F32),
        compiler_params=_params(2),
    )(sinks, q, k, k, v, v, x2, w_o, b_o)


def _block_ones(width, group):
    idx = np.arange(width) // group
    return jnp.asarray(idx[:, None] == idx[None, :], BF16)


def _chunk_tril(rows):
    idx = np.arange(rows)
    same = (idx[:, None] // CHUNK) == (idx[None, :] // CHUNK)
    return jnp.asarray(same & (idx[None, :] <= idx[:, None]), BF16)


def _rope_expand(lanes=128):
    half = ROT_DIM // 2
    dim = np.arange(lanes) % HEAD_DIM
    f = np.arange(half)[:, None]
    hit = (dim[None, :] % half == f)
    zero = np.zeros((half, lanes))
    cos_rows = np.concatenate([hit & (dim < ROT_DIM)[None, :], zero, zero], axis=1)
    sin_rows = np.concatenate([zero, -1.0 * (hit & (dim < half)[None, :]),
                               hit & ((dim >= half) & (dim < ROT_DIM))[None, :]], axis=1)
    return jnp.asarray(np.concatenate([cos_rows, sin_rows], axis=0), BF16)


def _row(v):
    return v.reshape(1, -1)


def kernel(x, positions, ab_norm_g, ab_w_in, conv_in_b, conv_dw_w, conv_dw_b, conv_ln_g, conv_ln_b, rwkv_mu, rwkv_w0, rwkv_w2, rwkv_a0, rwkv_a2, rwkv_g2, rwkv_k_k, rwkv_k_a, rwkv_r_k, rwkv_ln_g, rwkv_ln_b, ab_w_out, attn_norm_g, attn_w_qkv, attn_b_qkv, attn_q_norm_g, attn_k_norm_g, attn_sinks, attn_w_o, attn_b_o, ffn_norm_g, ffn_w_up, ffn_conv_w, ffn_conv_b, ffn_w_down):
    bsz, seq, d = x.shape
    depth = ffn_norm_g.shape[0]
    n = bsz * seq
    tt = 512
    ffn_tt = 256
    x2 = x.reshape(n, d)
    ones_rwkv = _block_ones(RWKV_W, RWKV_HEAD_DIM)
    ones_attn = _block_ones(4 * HEAD_DIM, HEAD_DIM)
    tri = _chunk_tril(256)
    rope_expand = _rope_expand()

    for layer in range(depth):
        i = layer // 2
        if layer % 2 == 0:
            pc, pr = _inproj(x2, _row(ab_norm_g[i]), ab_w_in[i].astype(BF16), tt)
            c = _convmod(pc, _row(conv_in_b[i]), conv_dw_w[i], _row(conv_dw_b[i]), _row(conv_ln_g[i]),
                         _row(conv_ln_b[i]), seq, tt)
            zeros = jnp.zeros((LORA_DECAY, RWKV_W), F32)
            w2a2 = jnp.concatenate([jnp.concatenate([rwkv_w2[i], zeros], axis=1),
                                    jnp.concatenate([zeros, rwkv_a2[i]], axis=1)], axis=0).astype(BF16)
            ar, bk, vv, wc, gate, bonus = _rwkv_prep(
                pr, _row(rwkv_mu[i]), _row(rwkv_w0[i]), w2a2, _row(rwkv_a0[i]), rwkv_g2[i].astype(BF16),
                _row(rwkv_k_k[i]), _row(rwkv_k_a[i]), _row(rwkv_r_k[i]), ones_rwkv, tri, bsz, seq, tt)
            y = _rwkv_core(ar, bk, vv, wc)
            x2 = _mixout(x2, c, y, gate, bonus, _row(rwkv_ln_g[i]), _row(rwkv_ln_b[i]), ones_rwkv,
                         ab_w_out[i].astype(BF16), tt)
        else:
            half = ROT_DIM // 2
            inv_freq = ROPE_THETA ** (-(jnp.arange(half, dtype=F32) * 2.0) / ROT_DIM)
            lanes = 128
            cs = _rope_table(inv_freq, positions.astype(F32).reshape(n // lanes, lanes))
            cs = cs.reshape(ROT_DIM, n).T
            q, k, v = _qkv(x2, cs, _row(attn_norm_g[i]), attn_w_qkv[i].astype(BF16), _row(attn_b_qkv[i]),
                           _row(jnp.tile(attn_q_norm_g[i], 4)), _row(jnp.tile(attn_k_norm_g[i], 4)),
                           rope_expand, ones_attn, bsz, seq, tt)
            x2 = _attention(attn_sinks[i], q, k, v, x2, attn_w_o[i].astype(BF16), _row(attn_b_o[i]), 4)
        x2 = _ffn(x2, _row(ffn_norm_g[layer]), ffn_w_up[layer].astype(BF16), ffn_conv_w[layer],
                  _row(ffn_conv_b[layer]), ffn_w_down[layer].astype(BF16), seq, ffn_tt)
    return x2.reshape(bsz, seq, d)
```

```python
import functools

import jax
import jax.numpy as jnp
import numpy as np
from jax import lax
from jax.experimental import pallas as pl
from jax.experimental.pallas import tpu as pltpu

F32 = jnp.float32
BF16 = jnp.bfloat16

CONV_W = 512
CONV_K = 31
CONV_LN_EPS = 1e-5
RWKV_HEADS = 8
RWKV_HEAD_DIM = 64
RWKV_W = RWKV_HEADS * RWKV_HEAD_DIM
LORA_DECAY = 64
LORA_ICLR = 64
LORA_GATE = 128
RWKV_GN_EPS = RWKV_HEAD_DIM * 1e-5
RWKV_IN = 3 * RWKV_W + LORA_DECAY + LORA_ICLR + LORA_GATE
HEAD_DIM = 64
N_HEADS = 16
N_KV_HEADS = 4
GROUP = N_HEADS // N_KV_HEADS
WINDOW = 128
ROT_DIM = 16
ROPE_THETA = 500000.0
D_FF = 2816
FFN_CONV_K = 3
NORM_EPS = 1e-6

CHUNK = 64
CORE_CHUNKS = 8
SUBLANES = 8
CONV_HALO = 32
V7X_VMEM_LIMIT = 56 * 1024 * 1024

NT_DIMS = (((1,), (1,)), ((), ()))
TN_DIMS = (((0,), (0,)), ((), ()))


def _params(n_axes=1):
    return pltpu.CompilerParams(dimension_semantics=("arbitrary",) * n_axes,
                                vmem_limit_bytes=V7X_VMEM_LIMIT)


def _const_spec(shape):
    nd = len(shape)
    return pl.BlockSpec(shape, lambda *_: (0,) * nd, pipeline_mode=pl.Buffered(1))


def _rms_norm(x, g):
    return x * lax.rsqrt(jnp.mean(x * x, axis=-1, keepdims=True) + NORM_EPS) * g


def _sigmoid(z):
    return 1.0 / (1.0 + jnp.exp(-z))


def _split3(z):
    hi = z.astype(BF16)
    r1 = z - hi.astype(F32)
    mid = r1.astype(BF16)
    lo = (r1 - mid.astype(F32)).astype(BF16)
    return hi, mid, lo


def _conv_shift_copies(ext_ref, tt):
    span = tt + CONV_HALO - SUBLANES
    for s in range(1, SUBLANES):
        ext_ref[s, 0:span, :] = ext_ref[0, pl.ds(s, span), :]


def _conv_taps(ext_ref, dww_ref, dwb_ref, lng_ref, lnb_ref, c_ref, tt, row_block):
    first = CONV_HALO - (CONV_K - 1)
    for r0 in range(0, tt, row_block):
        acc = jnp.broadcast_to(dwb_ref[...], (row_block, CONV_W))
        for s in range(SUBLANES):
            taps = [j for j in range(CONV_K) if (first + j) % SUBLANES == s]
            lo = first + taps[0] - s
            hi = first + taps[-1] - s
            win = ext_ref[s, pl.ds(r0 + lo, row_block + hi - lo), :]
            for j in taps:
                a = first + j - s - lo
                acc = acc + dww_ref[j:j + 1, :] * win[a:a + row_block]
        mu = jnp.mean(acc, axis=-1, keepdims=True)
        xc = acc - mu
        var = jnp.mean(xc * xc, axis=-1, keepdims=True)
        z = xc * lax.rsqrt(var + CONV_LN_EPS) * lng_ref[...] + lnb_ref[...]
        c_ref[r0:r0 + row_block, :] = (z * _sigmoid(z)).astype(c_ref.dtype)


def _convmod(pc, inb, dww, dwb, lng, lnb, seq, tt):
    n = pc.shape[0]
    tps = seq // tt
    hb = tt // CONV_HALO
    kern = functools.partial(_convmod_kernel, tiles_per_seq=tps, row_block=64)
    return pl.pallas_call(
        kern,
        grid=(n // tt,),
        in_specs=[pl.BlockSpec((tt, 2 * CONV_W), lambda i: (i, 0)),
                  pl.BlockSpec((CONV_HALO, 2 * CONV_W), lambda i: (jnp.maximum(i * hb - 1, 0), 0)),
                  _const_spec(inb.shape), _const_spec(dww.shape), _const_spec(dwb.shape),
                  _const_spec(lng.shape), _const_spec(lnb.shape)],
        out_specs=pl.BlockSpec((tt, CONV_W), lambda i: (i, 0)),
        out_shape=jax.ShapeDtypeStruct((n, CONV_W), BF16),
        scratch_shapes=[pltpu.VMEM((SUBLANES, tt + CONV_HALO, CONV_W), F32)],
        compiler_params=_params(),
    )(pc, pc, inb, dww, dwb, lng, lnb)


def _rwkv_prep_body(ext_ref, mu_ref, w0_ref, w2a2_ref, a0_ref, g2_ref, kk_ref, ka_ref, rk_ref, ones_ref, tri_ref,
                    ar_ref, bk_ref, v_ref, wc_ref, gate_ref, bonus_ref, tt):
    rw = ext_ref[SUBLANES:, :]
    rw = rw + (ext_ref[pl.ds(SUBLANES - 1, tt), :] - rw) * mu_ref[...]

    r = rw[:, 0:RWKV_W]
    k = rw[:, RWKV_W:2 * RWKV_W]
    v = rw[:, 2 * RWKV_W:3 * RWKV_W]
    wa = rw[:, 3 * RWKV_W:3 * RWKV_W + LORA_DECAY + LORA_ICLR]
    gd = rw[:, 3 * RWKV_W + LORA_DECAY + LORA_ICLR:]

    lane = lax.broadcasted_iota(jnp.int32, wa.shape, 1)
    z = jnp.where(lane < LORA_DECAY, jnp.tanh(wa), wa)
    proj = jnp.dot(z.astype(BF16), w2a2_ref[...], preferred_element_type=F32)
    zw = -(w0_ref[...] + proj[:, :RWKV_W])
    softplus = jnp.maximum(zw, 0.0) + jnp.log(1.0 + jnp.exp(-jnp.abs(zw)))
    lw = -jnp.exp(-softplus - 0.5)
    a = _sigmoid(a0_ref[...] + proj[:, RWKV_W:])
    gate_ref[...] = jnp.dot(_sigmoid(gd).astype(BF16), g2_ref[...], preferred_element_type=F32)

    kk = k * kk_ref[...]
    ss = jnp.dot((kk * kk).astype(BF16), ones_ref[...], preferred_element_type=F32)
    kk = kk / jnp.maximum(jnp.sqrt(ss), 1e-12)
    k = k * (1.0 + (a - 1.0) * ka_ref[...])
    rk = jnp.dot((r * k * rk_ref[...]).astype(BF16), ones_ref[...], preferred_element_type=F32)
    bonus_ref[...] = rk * v

    half = tri_ref.shape[0]
    cums = []
    for r0 in range(0, tt, half):
        parts = _split3(lw[r0:r0 + half])
        cums.append(sum(jnp.dot(tri_ref[...], p, preferred_element_type=F32) for p in parts))
    cum = jnp.concatenate(cums, axis=0)
    e_cum = jnp.exp(cum)
    e_inv = jnp.exp(-cum)
    rt = r * e_cum
    at = -kk * jnp.exp(cum - lw)
    kt = k * e_inv
    bt = kk * a * e_inv

    nch = tt // CHUNK
    for h in range(RWKV_HEADS):
        sl = slice(h * RWKV_HEAD_DIM, (h + 1) * RWKV_HEAD_DIM)
        for c in range(nch):
            rows = slice(c * CHUNK, (c + 1) * CHUNK)
            ar_ref[0, h, c, 0:CHUNK, :] = at[rows, sl].astype(BF16)
            ar_ref[0, h, c, CHUNK:, :] = rt[rows, sl].astype(BF16)
            bk_ref[0, h, c, 0:CHUNK, :] = bt[rows, sl].astype(BF16)
            bk_ref[0, h, c, CHUNK:, :] = kt[rows, sl].astype(BF16)
            v_ref[0, h, c, :, :] = v[rows, sl].astype(BF16)
            wc_ref[0, h, c, :, :] = e_cum[(c + 1) * CHUNK - 1:(c + 1) * CHUNK, sl]


def _front_kernel(x0_ref, xn_ref, ng_ref, win_ref, inb_ref, dww_ref, dwb_ref, lng_ref, lnb_ref,
                  mu_ref, w0_ref, w2a2_ref, a0_ref, g2_ref, kk_ref, ka_ref, rk_ref, ones_ref, tri_ref,
                  c_ref, ar_ref, bk_ref, v_ref, wc_ref, gate_ref, bonus_ref, ext_ref, prc_ref,
                  *, tiles_per_seq, row_block):
    i = pl.program_id(0)
    tt = xn_ref.shape[0]

    def project(x_ref):
        h = _rms_norm(x_ref[...], ng_ref[...])
        return jnp.dot(h.astype(BF16), win_ref[...], preferred_element_type=F32)

    def make_current(p, conv_hist, shift_hist):
        z = p[:, :2 * CONV_W] + inb_ref[...]
        ext_ref[0, 0:CONV_HALO, :] = conv_hist
        ext_ref[0, CONV_HALO:, :] = z[:, :CONV_W] * _sigmoid(z[:, CONV_W:])
        _conv_shift_copies(ext_ref, tt)
        prc_ref[0:SUBLANES, :] = shift_hist
        prc_ref[SUBLANES:, :] = p[:, 2 * CONV_W:]

    @pl.when(i == 0)
    def _():
        make_current(project(x0_ref), jnp.zeros((CONV_HALO, CONV_W), F32), jnp.zeros((SUBLANES, RWKV_IN), F32))

    p_next = project(xn_ref)
    _conv_taps(ext_ref, dww_ref, dwb_ref, lng_ref, lnb_ref, c_ref, tt, row_block)
    _rwkv_prep_body(prc_ref, mu_ref, w0_ref, w2a2_ref, a0_ref, g2_ref, kk_ref, ka_ref, rk_ref, ones_ref, tri_ref,
                    ar_ref, bk_ref, v_ref, wc_ref, gate_ref, bonus_ref, tt)
    same_seq = (i + 1) % tiles_per_seq != 0
    make_current(p_next,
                 jnp.where(same_seq, ext_ref[0, tt:tt + CONV_HALO, :], 0.0),
                 jnp.where(same_seq, prc_ref[tt:tt + SUBLANES, :], 0.0))


def _front(x2, ng, w_in, inb, dww, dwb, lng, lnb, mu, w0, w2a2, a0, g2, k_k, k_a, r_k, ones, tri, bsz, seq, tt):
    n, d = x2.shape
    tps = seq // tt
    steps = n // tt
    nch = tt // CHUNK
    nc = seq // CHUNK
    kern = functools.partial(_front_kernel, tiles_per_seq=tps, row_block=64)
    hm = lambda rows: pl.BlockSpec((1, RWKV_HEADS, nch, rows, RWKV_HEAD_DIM),
                                   lambda i: (i // tps, 0, i % tps, 0, 0))
    hm_shape = lambda rows, dt: jax.ShapeDtypeStruct((bsz, RWKV_HEADS, nc, rows, RWKV_HEAD_DIM), dt)
    consts = [ng, w_in, inb, dww, dwb, lng, lnb, mu, w0, w2a2, a0, g2, k_k, k_a, r_k, ones, tri]
    row = lambda w: pl.BlockSpec((tt, w), lambda i: (i, 0))
    return pl.pallas_call(
        kern,
        grid=(steps,),
        in_specs=[pl.BlockSpec((tt, d), lambda i: (0, 0)),
                  pl.BlockSpec((tt, d), lambda i: (jnp.minimum(i + 1, steps - 1), 0))]
                 + [_const_spec(c.shape) for c in consts],
        out_specs=[row(CONV_W), hm(2 * CHUNK), hm(2 * CHUNK), hm(CHUNK), hm(1), row(RWKV_W), row(RWKV_W)],
        out_shape=[jax.ShapeDtypeStruct((n, CONV_W), BF16),
                   hm_shape(2 * CHUNK, BF16), hm_shape(2 * CHUNK, BF16), hm_shape(CHUNK, BF16), hm_shape(1, F32),
                   jax.ShapeDtypeStruct((n, RWKV_W), F32), jax.ShapeDtypeStruct((n, RWKV_W), F32)],
        scratch_shapes=[pltpu.VMEM((SUBLANES, tt + CONV_HALO, CONV_W), F32),
                        pltpu.VMEM((tt + SUBLANES, RWKV_IN), F32)],
        compiler_params=_params(),
    )(x2, x2, *consts)


def _rwkv_core_kernel(ar_ref, bk_ref, v_ref, wc_ref, y_ref, ht_ref):
    @pl.when(pl.program_id(1) == 0)
    def _():
        ht_ref[...] = jnp.zeros_like(ht_ref)

    two = 2 * CHUNK
    row = lax.broadcasted_iota(jnp.int32, (two, two), 0)
    col = lax.broadcasted_iota(jnp.int32, (two, two), 1)
    rm = row % CHUNK
    cm = col % CHUNK
    mask = (cm < rm) | ((row >= CHUNK) & (cm == rm))
    eye = (lax.broadcasted_iota(jnp.int32, (CHUNK, CHUNK), 0)
           == lax.broadcasted_iota(jnp.int32, (CHUNK, CHUNK), 1)).astype(F32)
    bmm = functools.partial(jnp.einsum, preferred_element_type=F32)
    nb = RWKV_HEADS * CORE_CHUNKS
    ar = ar_ref[0].reshape(nb, two, RWKV_HEAD_DIM)
    bk = bk_ref[0].reshape(nb, two, RWKV_HEAD_DIM)
    vv = v_ref[0].reshape(nb, CHUNK, RWKV_HEAD_DIM)
    s = jnp.where(mask[None], bmm('bmk,bnk->bmn', ar, bk), 0.0)
    a_ab = s[:, 0:CHUNK, 0:CHUNK]
    p = eye[None] + a_ab
    qb = a_ab.astype(BF16)
    q = bmm('bij,bjk->bik', qb, qb)
    for _ in range(4):
        qb = q.astype(BF16)
        both = bmm('bij,bjk->bik', jnp.concatenate([qb, p.astype(BF16)], axis=1), qb)
        q = both[:, 0:CHUNK]
        p = p + both[:, CHUNK:]
    p = p + bmm('bij,bjk->bik', p.astype(BF16), q.astype(BF16))
    sb = s.astype(BF16)
    av = bmm('bij,bjk->bik', sb[:, 0:CHUNK], jnp.concatenate([jnp.zeros_like(vv), vv], axis=1))

    def per_chunk(z, g):
        return z.reshape((RWKV_HEADS, CORE_CHUNKS) + z.shape[1:])[:, g]

    tb = p.astype(BF16)
    for g in range(CORE_CHUNKS):
        ht = ht_ref[...]
        bk_g = per_chunk(bk, g)
        vv_g = per_chunk(vv, g)
        sb_g = per_chunk(sb, g)
        arh = bmm('hmk,hnk->hmn', per_chunk(ar, g), ht.astype(BF16))
        u = bmm('hij,hjk->hik', per_chunk(tb, g), (arh[:, 0:CHUNK] + per_chunk(av, g)).astype(BF16))
        uv = jnp.concatenate([u.astype(BF16), vv_g], axis=1)
        y = arh[:, CHUNK:] + bmm('hij,hjk->hik', sb_g[:, CHUNK:], uv)
        upd = bmm('hsi,hsj->hij', uv, bk_g)
        ht_ref[...] = wc_ref[0, :, g] * (ht + upd)
        y_ref[g * CHUNK:(g + 1) * CHUNK, :] = jnp.concatenate([y[h] for h in range(RWKV_HEADS)], axis=1)


def _rwkv_core(ar, bk, v, wc):
    bsz, nh, nc = ar.shape[:3]
    g = CORE_CHUNKS
    steps = nc // g
    hm = lambda rows: pl.BlockSpec((1, nh, g, rows, RWKV_HEAD_DIM), lambda b, i: (b, 0, i, 0, 0))
    return pl.pallas_call(
        _rwkv_core_kernel,
        grid=(bsz, steps),
        in_specs=[hm(2 * CHUNK), hm(2 * CHUNK), hm(CHUNK), hm(1)],
        out_specs=pl.BlockSpec((g * CHUNK, RWKV_W), lambda b, i: (b * steps + i, 0)),
        out_shape=jax.ShapeDtypeStruct((bsz * nc * CHUNK, RWKV_W), F32),
        scratch_shapes=[pltpu.VMEM((nh, RWKV_HEAD_DIM, RWKV_HEAD_DIM), F32)],
        compiler_params=_params(2),
    )(ar, bk, v, wc)


def _mixout_kernel(x_ref, c_ref, y_ref, gate_ref, bonus_ref, lng_ref, lnb_ref, ones_ref, w_ref, o_ref):
    inv_n = 1.0 / RWKV_HEAD_DIM
    y = y_ref[...]
    hi, mid, _ = _split3(y)
    mu = (jnp.dot(hi, ones_ref[...], preferred_element_type=F32)
          + jnp.dot(mid, ones_ref[...], preferred_element_type=F32)) * inv_n
    yc = y - mu
    var = jnp.dot((yc * yc).astype(BF16), ones_ref[...], preferred_element_type=F32) * inv_n
    yn = yc * lax.rsqrt(var + RWKV_GN_EPS) * lng_ref[...] + lnb_ref[...]
    yo = ((yn + bonus_ref[...]) * gate_ref[...]).astype(BF16)
    mix = (jnp.dot(c_ref[...], w_ref[0:CONV_W, :], preferred_element_type=F32)
           + jnp.dot(yo, w_ref[CONV_W:, :], preferred_element_type=F32))
    o_ref[...] = x_ref[...] + mix


def _mixout(x2, c, y, gate, bonus, lng, lnb, ones, w_out, tt):
    n, d = x2.shape
    row = lambda w: pl.BlockSpec((tt, w), lambda i: (i, 0))
    consts = [lng, lnb, ones, w_out]
    return pl.pallas_call(
        _mixout_kernel,
        grid=(n // tt,),
        in_specs=[row(d), row(CONV_W), row(RWKV_W), row(RWKV_W), row(RWKV_W)] + [_const_spec(c_.shape) for c_ in consts],
        out_specs=row(d),
        out_shape=jax.ShapeDtypeStruct((n, d), F32),
        compiler_params=_params(),
    )(x2, c, y, gate, bonus, *consts)


def _ffn_kernel(x_ref, halo_ref, g_ref, wup_ref, cw_ref, cb_ref, wdn_ref, o_ref, gs_ref, *, tiles_per_seq):
    tt = x_ref.shape[0]
    keep = (pl.program_id(0) % tiles_per_seq != 0).astype(F32)
    x = x_ref[...]
    xe = jnp.concatenate([halo_ref[...] * keep, x], axis=0)
    h = _rms_norm(xe, g_ref[...]).astype(BF16)
    u = jnp.dot(h, wup_ref[...], preferred_element_type=F32)
    gs_ref[...] = u[:, :D_FF]
    gate = cb_ref[...]
    for j in range(FFN_CONV_K):
        gate = gate + cw_ref[j:j + 1, :] * gs_ref[pl.ds(SUBLANES - (FFN_CONV_K - 1) + j, tt), :]
    act = (gate * _sigmoid(gate) * u[SUBLANES:, D_FF:]).astype(BF16)
    o_ref[...] = x + jnp.dot(act, wdn_ref[...], preferred_element_type=F32)


def _layer_spec(stacked, layer):
    tail = stacked.shape[1:]
    return pl.BlockSpec((None,) + tail, lambda *_: (layer,) + (0,) * len(tail), pipeline_mode=pl.Buffered(1))


def _ffn(x2, g, wup, cw, cb, wdn, layer, seq, tt):
    n, d = x2.shape
    tps = seq // tt
    hb = tt // SUBLANES
    kern = functools.partial(_ffn_kernel, tiles_per_seq=tps)
    consts = [g, wup, cw, cb, wdn]
    return pl.pallas_call(
        kern,
        grid=(n // tt,),
        in_specs=[pl.BlockSpec((tt, d), lambda i: (i, 0)),
                  pl.BlockSpec((SUBLANES, d), lambda i: (jnp.maximum(i * hb - 1, 0), 0))]
                 + [_layer_spec(c, layer) for c in consts],
        out_specs=pl.BlockSpec((tt, d), lambda i: (i, 0)),
        out_shape=jax.ShapeDtypeStruct((n, d), F32),
        scratch_shapes=[pltpu.VMEM((tt + SUBLANES, D_FF), F32)],
        compiler_params=_params(),
    )(x2, x2, *consts)


def _rope_table_kernel(freq_ref, pos_ref, cs_ref):
    half = ROT_DIM // 2
    pos = pos_ref[...]
    for f in range(half):
        ang = pos * freq_ref[f]
        cs_ref[f] = jnp.cos(ang)
        cs_ref[half + f] = jnp.sin(ang)


def _rope_table(inv_freq, pos_dense):
    rows, lanes = pos_dense.shape
    return pl.pallas_call(
        _rope_table_kernel,
        in_specs=[pl.BlockSpec(memory_space=pltpu.SMEM), pl.BlockSpec((rows, lanes), lambda: (0, 0))],
        out_specs=pl.BlockSpec((ROT_DIM, rows, lanes), lambda: (0, 0, 0)),
        out_shape=jax.ShapeDtypeStruct((ROT_DIM, rows, lanes), F32),
    )(inv_freq, pos_dense)


def _qkv_kernel(x_ref, cs_ref, g_ref, w_ref, b_ref, qg_ref, kg_ref, expand_ref, ones_ref,
                q_ref, k_ref, v_ref):
    qd = N_HEADS * HEAD_DIM
    kd = N_KV_HEADS * HEAD_DIM
    h = _rms_norm(x_ref[...], g_ref[...]).astype(BF16)
    qkv = jnp.dot(h, w_ref[...], preferred_element_type=F32) + b_ref[...]

    lanes = expand_ref.shape[1] // 3
    cs_hi, cs_lo, _ = _split3(cs_ref[...])
    tab = (jnp.dot(cs_hi, expand_ref[...], preferred_element_type=F32)
           + jnp.dot(cs_lo, expand_ref[...], preferred_element_type=F32))
    dim = lax.broadcasted_iota(jnp.int32, (1, lanes), 1) % HEAD_DIM
    cos = tab[:, :lanes] + (dim >= ROT_DIM).astype(F32)
    sin_a = tab[:, lanes:2 * lanes]
    sin_b = tab[:, 2 * lanes:]
    half = ROT_DIM // 2

    def norm_rope(z, gain, scale):
        ss = jnp.dot((z * z).astype(BF16), ones_ref[...], preferred_element_type=F32)
        z = z * lax.rsqrt(ss * (1.0 / HEAD_DIM) + NORM_EPS) * gain
        outs = []
        for c0 in range(0, z.shape[1], lanes):
            zz = z[:, c0:c0 + lanes]
            rot = (zz * cos + pltpu.roll(zz, lanes - half, 1) * sin_a + pltpu.roll(zz, half, 1) * sin_b)
            outs.append(rot * scale)
        return outs

    blk = 4 * HEAD_DIM
    for c0 in range(0, qd, blk):
        outs = norm_rope(qkv[:, c0:c0 + blk], qg_ref[...], HEAD_DIM ** -0.5)
        for j, o in enumerate(outs):
            hh = c0 // HEAD_DIM + 2 * j
            q_ref[0, hh] = o[:, :HEAD_DIM].astype(BF16)
            q_ref[0, hh + 1] = o[:, HEAD_DIM:].astype(BF16)
    outs = norm_rope(qkv[:, qd:qd + kd], kg_ref[...], 1.0)
    for j, o in enumerate(outs):
        k_ref[0, 2 * j] = o[:, :HEAD_DIM].astype(BF16)
        k_ref[0, 2 * j + 1] = o[:, HEAD_DIM:].astype(BF16)
    vv = qkv[:, qd + kd:]
    for j in range(N_KV_HEADS):
        v_ref[0, j] = vv[:, j * HEAD_DIM:(j + 1) * HEAD_DIM].astype(BF16)


def _qkv(x2, cs, g, w, b, qg, kg, expand, ones, bsz, seq, tt):
    n, d = x2.shape
    tps = seq // tt
    consts = [g, w, b, qg, kg, expand, ones]
    hm = lambda nh: pl.BlockSpec((1, nh, tt, HEAD_DIM), lambda i: (i // tps, 0, i % tps, 0))
    return pl.pallas_call(
        _qkv_kernel,
        grid=(n // tt,),
        in_specs=[pl.BlockSpec((tt, d), lambda i: (i, 0)), pl.BlockSpec((tt, ROT_DIM), lambda i: (i, 0))]
                 + [_const_spec(c.shape) for c in consts],
        out_specs=[hm(N_HEADS), hm(N_KV_HEADS), hm(N_KV_HEADS)],
        out_shape=[jax.ShapeDtypeStruct((bsz, N_HEADS, seq, HEAD_DIM), BF16),
                   jax.ShapeDtypeStruct((bsz, N_KV_HEADS, seq, HEAD_DIM), BF16),
                   jax.ShapeDtypeStruct((bsz, N_KV_HEADS, seq, HEAD_DIM), BF16)],
        compiler_params=_params(),
    )(x2, cs, *consts)


def _attn_kernel(sink_ref, q_ref, kc_ref, kp_ref, vc_ref, vp_ref, x_ref, w_ref, b_ref, out_ref, *, q_blocks):
    i = pl.program_id(1)
    blk = WINDOW
    rows = GROUP * blk
    t = lax.broadcasted_iota(jnp.int32, (rows, 2 * blk), 0) % blk
    c = lax.broadcasted_iota(jnp.int32, (rows, 2 * blk), 1)
    d = c - t
    band = (d >= 1) & (d <= blk)
    head_of_row = lax.broadcasted_iota(jnp.int32, (rows, 1), 0) // blk
    blocks = []
    for qb in range(q_blocks):
        if qb == 0:
            valid = band & ((c >= blk) | (i > 0))
        else:
            valid = band
        outs = []
        for g in range(N_KV_HEADS):
            q = q_ref[0, g * GROUP:(g + 1) * GROUP, qb * blk:(qb + 1) * blk, :].reshape(rows, HEAD_DIM)
            if qb == 0:
                kprev, vprev = kp_ref[0, g], vp_ref[0, g]
            else:
                kprev = kc_ref[0, g, (qb - 1) * blk:qb * blk, :]
                vprev = vc_ref[0, g, (qb - 1) * blk:qb * blk, :]
            k = jnp.concatenate([kprev, kc_ref[0, g, qb * blk:(qb + 1) * blk, :]], axis=0)
            v = jnp.concatenate([vprev, vc_ref[0, g, qb * blk:(qb + 1) * blk, :]], axis=0)
            s = lax.dot_general(q, k, NT_DIMS, preferred_element_type=F32)
            s = jnp.where(valid, s, -jnp.inf)
            sink = jnp.zeros((rows, 1), F32)
            for j in range(GROUP):
                sink = jnp.where(head_of_row == j, sink_ref[g * GROUP + j], sink)
            m = jnp.maximum(jnp.max(s, axis=-1, keepdims=True), sink)
            p = jnp.exp(s - m)
            den = jnp.sum(p, axis=-1, keepdims=True) + jnp.exp(sink - m)
            o = jnp.dot(p.astype(BF16), v, preferred_element_type=F32) / den
            outs.extend(o[j * blk:(j + 1) * blk] for j in range(GROUP))
        blocks.append(jnp.concatenate(outs, axis=1).astype(BF16))
    o_all = jnp.concatenate(blocks, axis=0)
    out_ref[...] = x_ref[...] + jnp.dot(o_all, w_ref[...], preferred_element_type=F32) + b_ref[...]


def _attention(sinks, q, k, v, x2, w_o, b_o, q_blocks):
    bsz, _, seq, _ = q.shape
    d = x2.shape[1]
    tq = q_blocks * WINDOW
    steps = seq // tq
    kern = functools.partial(_attn_kernel, q_blocks=q_blocks)
    cur = lambda nh: pl.BlockSpec((1, nh, tq, HEAD_DIM), lambda b, i: (b, 0, i, 0))
    prev = pl.BlockSpec((1, N_KV_HEADS, WINDOW, HEAD_DIM), lambda b, i: (b, 0, jnp.maximum(i * q_blocks - 1, 0), 0))
    rows = pl.BlockSpec((tq, d), lambda b, i: (b * steps + i, 0))
    return pl.pallas_call(
        kern,
        grid=(bsz, steps),
        in_specs=[pl.BlockSpec(memory_space=pltpu.SMEM), cur(N_HEADS), cur(N_KV_HEADS), prev, cur(N_KV_HEADS), prev,
                  rows, _const_spec(w_o.shape), _const_spec(b_o.shape)],
        out_specs=rows,
        out_shape=jax.ShapeDtypeStruct((bsz * seq, d), F32),
        compiler_params=_params(2),
    )(sinks, q, k, k, v, v, x2, w_o, b_o)


def _block_ones(width, group):
    idx = np.arange(width) // group
    return jnp.asarray(idx[:, None] == idx[None, :], BF16)


def _chunk_tril(rows):
    idx = np.arange(rows)
    same = (idx[:, None] // CHUNK) == (idx[None, :] // CHUNK)
    return jnp.asarray(same & (idx[None, :] <= idx[:, None]), BF16)


def _rope_expand(lanes=128):
    half = ROT_DIM // 2
    dim = np.arange(lanes) % HEAD_DIM
    f = np.arange(half)[:, None]
    hit = (dim[None, :] % half == f)
    zero = np.zeros((half, lanes))
    cos_rows = np.concatenate([hit & (dim < ROT_DIM)[None, :], zero, zero], axis=1)
    sin_rows = np.concatenate([zero, -1.0 * (hit & (dim < half)[None, :]),
                               hit & ((dim >= half) & (dim < ROT_DIM))[None, :]], axis=1)
    return jnp.asarray(np.concatenate([cos_rows, sin_rows], axis=0), BF16)


def _row(v):
    return v.reshape(1, -1)


def kernel(x, positions, ab_norm_g, ab_w_in, conv_in_b, conv_dw_w, conv_dw_b, conv_ln_g, conv_ln_b, rwkv_mu, rwkv_w0, rwkv_w2, rwkv_a0, rwkv_a2, rwkv_g2, rwkv_k_k, rwkv_k_a, rwkv_r_k, rwkv_ln_g, rwkv_ln_b, ab_w_out, attn_norm_g, attn_w_qkv, attn_b_qkv, attn_q_norm_g, attn_k_norm_g, attn_sinks, attn_w_o, attn_b_o, ffn_norm_g, ffn_w_up, ffn_conv_w, ffn_conv_b, ffn_w_down):
    bsz, seq, d = x.shape
    depth = ffn_norm_g.shape[0]
    n = bsz * seq
    tt = 512
    ffn_tt = 256
    x2 = x.reshape(n, d)
    ones_rwkv = _block_ones(RWKV_W, RWKV_HEAD_DIM)
    ones_attn = _block_ones(4 * HEAD_DIM, HEAD_DIM)
    tri = _chunk_tril(256)
    rope_expand = _rope_expand()
    ffn_g = ffn_norm_g[:, None, :]
    ffn_cb = ffn_conv_b[:, None, :]
    ffn_up = ffn_w_up.astype(BF16)
    ffn_down = ffn_w_down.astype(BF16)

    for layer in range(depth):
        i = layer // 2
        if layer % 2 == 0:
            zeros = jnp.zeros((LORA_DECAY, RWKV_W), F32)
            w2a2 = jnp.concatenate([jnp.concatenate([rwkv_w2[i], zeros], axis=1),
                                    jnp.concatenate([zeros, rwkv_a2[i]], axis=1)], axis=0).astype(BF16)
            c, ar, bk, vv, wc, gate, bonus = _front(
                x2, _row(ab_norm_g[i]), ab_w_in[i].astype(BF16),
                _row(conv_in_b[i]), conv_dw_w[i], _row(conv_dw_b[i]), _row(conv_ln_g[i]), _row(conv_ln_b[i]),
                _row(rwkv_mu[i]), _row(rwkv_w0[i]), w2a2, _row(rwkv_a0[i]), rwkv_g2[i].astype(BF16),
                _row(rwkv_k_k[i]), _row(rwkv_k_a[i]), _row(rwkv_r_k[i]), ones_rwkv, tri, bsz, seq, tt)
            y = _rwkv_core(ar, bk, vv, wc)
            x2 = _mixout(x2, c, y, gate, bonus, _row(rwkv_ln_g[i]), _row(rwkv_ln_b[i]), ones_rwkv,
                         ab_w_out[i].astype(BF16), tt)
        else:
            half = ROT_DIM // 2
            inv_freq = ROPE_THETA ** (-(jnp.arange(half, dtype=F32) * 2.0) / ROT_DIM)
            lanes = 128
            cs = _rope_table(inv_freq, positions.astype(F32).reshape(n // lanes, lanes))
            cs = cs.reshape(ROT_DIM, n).T
            q, k, v = _qkv(x2, cs, _row(attn_norm_g[i]), attn_w_qkv[i].astype(BF16), _row(attn_b_qkv[i]),
                           _row(jnp.tile(attn_q_norm_g[i], 4)), _row(jnp.tile(attn_k_norm_g[i], 4)),
                           rope_expand, ones_attn, bsz, seq, tt)
            x2 = _attention(attn_sinks[i], q, k, v, x2, attn_w_o[i].astype(BF16), _row(attn_b_o[i]), 4)
        x2 = _ffn(x2, ffn_g, ffn_up, ffn_conv_w, ffn_cb, ffn_down, layer, seq, ffn_tt)
    return x2.reshape(bsz, seq, d)
```

```python
import functools

import jax
import jax.numpy as jnp
import numpy as np
from jax import lax
from jax.experimental import pallas as pl
from jax.experimental.pallas import tpu as pltpu

F32 = jnp.float32
BF16 = jnp.bfloat16

CONV_W = 512
CONV_K = 31
CONV_LN_EPS = 1e-5
RWKV_HEADS = 8
RWKV_HEAD_DIM = 64
RWKV_W = RWKV_HEADS * RWKV_HEAD_DIM
LORA_DECAY = 64
LORA_ICLR = 64
LORA_GATE = 128
RWKV_GN_EPS = RWKV_HEAD_DIM * 1e-5
RWKV_IN = 3 * RWKV_W + LORA_DECAY + LORA_ICLR + LORA_GATE
HEAD_DIM = 64
N_HEADS = 16
N_KV_HEADS = 4
GROUP = N_HEADS // N_KV_HEADS
WINDOW = 128
ROT_DIM = 16
ROPE_THETA = 500000.0
D_FF = 2816
FFN_CONV_K = 3
NORM_EPS = 1e-6

CHUNK = 64
CORE_CHUNKS = 16
CORE_SPAN = 2
CORE_GROUP = 4
SUBLANES = 8
CONV_HALO = 32
V7X_VMEM_LIMIT = 56 * 1024 * 1024

NT_DIMS = (((1,), (1,)), ((), ()))
TN_DIMS = (((0,), (0,)), ((), ()))


def _params(n_axes=1):
    return pltpu.CompilerParams(dimension_semantics=("arbitrary",) * n_axes,
                                vmem_limit_bytes=V7X_VMEM_LIMIT)


def _const_spec(shape):
    nd = len(shape)
    return pl.BlockSpec(shape, lambda *_: (0,) * nd, pipeline_mode=pl.Buffered(1))


def _rms_norm(x, g):
    return x * lax.rsqrt(jnp.mean(x * x, axis=-1, keepdims=True) + NORM_EPS) * g


def _sigmoid(z):
    return 1.0 / (1.0 + jnp.exp(-z))


def _split3(z):
    hi = z.astype(BF16)
    r1 = z - hi.astype(F32)
    mid = r1.astype(BF16)
    lo = (r1 - mid.astype(F32)).astype(BF16)
    return hi, mid, lo


def _conv_shift_copies(ext_ref, tt):
    span = tt + CONV_HALO - SUBLANES
    for s in range(1, SUBLANES):
        ext_ref[s, 0:span, :] = ext_ref[0, pl.ds(s, span), :]


def _conv_taps(ext_ref, dww_ref, dwb_ref, lng_ref, lnb_ref, c_ref, tt, row_block):
    first = CONV_HALO - (CONV_K - 1)
    for r0 in range(0, tt, row_block):
        acc = jnp.broadcast_to(dwb_ref[...], (row_block, CONV_W))
        for s in range(SUBLANES):
            taps = [j for j in range(CONV_K) if (first + j) % SUBLANES == s]
            lo = first + taps[0] - s
            hi = first + taps[-1] - s
            win = ext_ref[s, pl.ds(r0 + lo, row_block + hi - lo), :]
            for j in taps:
                a = first + j - s - lo
                acc = acc + dww_ref[j:j + 1, :] * win[a:a + row_block]
        mu = jnp.mean(acc, axis=-1, keepdims=True)
        xc = acc - mu
        var = jnp.mean(xc * xc, axis=-1, keepdims=True)
        z = xc * lax.rsqrt(var + CONV_LN_EPS) * lng_ref[...] + lnb_ref[...]
        c_ref[r0:r0 + row_block, :] = (z * _sigmoid(z)).astype(c_ref.dtype)


def _convmod(pc, inb, dww, dwb, lng, lnb, seq, tt):
    n = pc.shape[0]
    tps = seq // tt
    hb = tt // CONV_HALO
    kern = functools.partial(_convmod_kernel, tiles_per_seq=tps, row_block=64)
    return pl.pallas_call(
        kern,
        grid=(n // tt,),
        in_specs=[pl.BlockSpec((tt, 2 * CONV_W), lambda i: (i, 0)),
                  pl.BlockSpec((CONV_HALO, 2 * CONV_W), lambda i: (jnp.maximum(i * hb - 1, 0), 0)),
                  _const_spec(inb.shape), _const_spec(dww.shape), _const_spec(dwb.shape),
                  _const_spec(lng.shape), _const_spec(lnb.shape)],
        out_specs=pl.BlockSpec((tt, CONV_W), lambda i: (i, 0)),
        out_shape=jax.ShapeDtypeStruct((n, CONV_W), BF16),
        scratch_shapes=[pltpu.VMEM((SUBLANES, tt + CONV_HALO, CONV_W), F32)],
        compiler_params=_params(),
    )(pc, pc, inb, dww, dwb, lng, lnb)


def _rwkv_prep_body(ext_ref, mu_ref, w0_ref, w2a2_ref, a0_ref, g2_ref, kk_ref, ka_ref, rk_ref, ones_ref, tri_ref,
                    ar_ref, bk_ref, v_ref, wc_ref, gate_ref, bonus_ref, tt):
    rw = ext_ref[SUBLANES:, :]
    rw = rw + (ext_ref[pl.ds(SUBLANES - 1, tt), :] - rw) * mu_ref[...]

    r = rw[:, 0:RWKV_W]
    k = rw[:, RWKV_W:2 * RWKV_W]
    v = rw[:, 2 * RWKV_W:3 * RWKV_W]
    wa = rw[:, 3 * RWKV_W:3 * RWKV_W + LORA_DECAY + LORA_ICLR]
    gd = rw[:, 3 * RWKV_W + LORA_DECAY + LORA_ICLR:]

    lane = lax.broadcasted_iota(jnp.int32, wa.shape, 1)
    z = jnp.where(lane < LORA_DECAY, jnp.tanh(wa), wa)
    proj = jnp.dot(z.astype(BF16), w2a2_ref[...], preferred_element_type=F32)
    zw = -(w0_ref[...] + proj[:, :RWKV_W])
    softplus = jnp.maximum(zw, 0.0) + jnp.log(1.0 + jnp.exp(-jnp.abs(zw)))
    lw = -jnp.exp(-softplus - 0.5)
    a = _sigmoid(a0_ref[...] + proj[:, RWKV_W:])
    gate_ref[...] = jnp.dot(_sigmoid(gd).astype(BF16), g2_ref[...], preferred_element_type=F32)

    kk = k * kk_ref[...]
    ss = jnp.dot((kk * kk).astype(BF16), ones_ref[...], preferred_element_type=F32)
    kk = kk / jnp.maximum(jnp.sqrt(ss), 1e-12)
    k = k * (1.0 + (a - 1.0) * ka_ref[...])
    rk = jnp.dot((r * k * rk_ref[...]).astype(BF16), ones_ref[...], preferred_element_type=F32)
    bonus_ref[...] = rk * v

    half = tri_ref.shape[0]
    cums = []
    for r0 in range(0, tt, half):
        parts = _split3(lw[r0:r0 + half])
        cums.append(sum(jnp.dot(tri_ref[...], p, preferred_element_type=F32) for p in parts))
    cum = jnp.concatenate(cums, axis=0)
    e_cum = jnp.exp(cum)
    e_inv = jnp.exp(-cum)
    rt = r * e_cum
    at = -kk * jnp.exp(cum - lw)
    kt = k * e_inv
    bt = kk * a * e_inv

    nch = tt // CHUNK
    for c in range(nch):
        rows = slice(c * CHUNK, (c + 1) * CHUNK)
        ar_ref[0, c, 0:CHUNK, :] = at[rows].astype(BF16)
        ar_ref[0, c, CHUNK:, :] = rt[rows].astype(BF16)
        bk_ref[0, c, 0:CHUNK, :] = bt[rows].astype(BF16)
        bk_ref[0, c, CHUNK:, :] = kt[rows].astype(BF16)
        v_ref[0, c, :, :] = v[rows].astype(BF16)
        wc_ref[0, c, :, :] = jnp.broadcast_to(e_cum[(c + 1) * CHUNK - 1:(c + 1) * CHUNK], (SUBLANES, RWKV_W))


def _front_kernel(x0_ref, xn_ref, ng_ref, win_ref, inb_ref, dww_ref, dwb_ref, lng_ref, lnb_ref,
                  mu_ref, w0_ref, w2a2_ref, a0_ref, g2_ref, kk_ref, ka_ref, rk_ref, ones_ref, tri_ref,
                  c_ref, ar_ref, bk_ref, v_ref, wc_ref, gate_ref, bonus_ref, ext_ref, prc_ref,
                  *, tiles_per_seq, row_block):
    i = pl.program_id(0)
    tt = xn_ref.shape[0]

    def project(x_ref):
        h = _rms_norm(x_ref[...], ng_ref[...])
        return jnp.dot(h.astype(BF16), win_ref[...], preferred_element_type=F32)

    def make_current(p, conv_hist, shift_hist):
        z = p[:, :2 * CONV_W] + inb_ref[...]
        ext_ref[0, 0:CONV_HALO, :] = conv_hist
        ext_ref[0, CONV_HALO:, :] = z[:, :CONV_W] * _sigmoid(z[:, CONV_W:])
        _conv_shift_copies(ext_ref, tt)
        prc_ref[0:SUBLANES, :] = shift_hist
        prc_ref[SUBLANES:, :] = p[:, 2 * CONV_W:]

    @pl.when(i == 0)
    def _():
        make_current(project(x0_ref), jnp.zeros((CONV_HALO, CONV_W), F32), jnp.zeros((SUBLANES, RWKV_IN), F32))

    p_next = project(xn_ref)
    _conv_taps(ext_ref, dww_ref, dwb_ref, lng_ref, lnb_ref, c_ref, tt, row_block)
    _rwkv_prep_body(prc_ref, mu_ref, w0_ref, w2a2_ref, a0_ref, g2_ref, kk_ref, ka_ref, rk_ref, ones_ref, tri_ref,
                    ar_ref, bk_ref, v_ref, wc_ref, gate_ref, bonus_ref, tt)
    same_seq = (i + 1) % tiles_per_seq != 0
    make_current(p_next,
                 jnp.where(same_seq, ext_ref[0, tt:tt + CONV_HALO, :], 0.0),
                 jnp.where(same_seq, prc_ref[tt:tt + SUBLANES, :], 0.0))


def _front(x2, ng, w_in, inb, dww, dwb, lng, lnb, mu, w0, w2a2, a0, g2, k_k, k_a, r_k, ones, tri, bsz, seq, tt):
    n, d = x2.shape
    tps = seq // tt
    steps = n // tt
    nch = tt // CHUNK
    nc = seq // CHUNK
    kern = functools.partial(_front_kernel, tiles_per_seq=tps, row_block=64)
    hm = lambda rows: pl.BlockSpec((1, nch, rows, RWKV_W), lambda i: (i // tps, i % tps, 0, 0))
    hm_shape = lambda rows, dt: jax.ShapeDtypeStruct((bsz, nc, rows, RWKV_W), dt)
    consts = [ng, w_in, inb, dww, dwb, lng, lnb, mu, w0, w2a2, a0, g2, k_k, k_a, r_k, ones, tri]
    row = lambda w: pl.BlockSpec((tt, w), lambda i: (i, 0))
    return pl.pallas_call(
        kern,
        grid=(steps,),
        in_specs=[pl.BlockSpec((tt, d), lambda i: (0, 0)),
                  pl.BlockSpec((tt, d), lambda i: (jnp.minimum(i + 1, steps - 1), 0))]
                 + [_const_spec(c.shape) for c in consts],
        out_specs=[row(CONV_W), hm(2 * CHUNK), hm(2 * CHUNK), hm(CHUNK), hm(SUBLANES), row(RWKV_W), row(RWKV_W)],
        out_shape=[jax.ShapeDtypeStruct((n, CONV_W), BF16),
                   hm_shape(2 * CHUNK, BF16), hm_shape(2 * CHUNK, BF16), hm_shape(CHUNK, BF16), hm_shape(SUBLANES, F32),
                   jax.ShapeDtypeStruct((n, RWKV_W), F32), jax.ShapeDtypeStruct((n, RWKV_W), F32)],
        scratch_shapes=[pltpu.VMEM((SUBLANES, tt + CONV_HALO, CONV_W), F32),
                        pltpu.VMEM((tt + SUBLANES, RWKV_IN), F32)],
        compiler_params=_params(),
    )(x2, x2, *consts)


def _rwkv_core_kernel(ar_ref, bk_ref, v_ref, wc_ref, y_ref, ht_ref):
    @pl.when(pl.program_id(0) == 0)
    def _():
        ht_ref[...] = jnp.zeros_like(ht_ref)

    bsz, nchunks = ar_ref.shape[0], ar_ref.shape[1]
    gw = CORE_GROUP * RWKV_HEAD_DIM
    ngroups = RWKV_W // gw
    npar = bsz * ngroups
    row = lax.broadcasted_iota(jnp.int32, (CHUNK, gw), 0)
    col = lax.broadcasted_iota(jnp.int32, (CHUNK, gw), 1) % RWKV_HEAD_DIM
    strict = col < row
    incl = col <= row
    eye = (col == row).astype(F32)
    blk_r = lax.broadcasted_iota(jnp.int32, (gw, gw), 0) // RWKV_HEAD_DIM
    blk_c = lax.broadcasted_iota(jnp.int32, (gw, gw), 1) // RWKV_HEAD_DIM
    same_head = blk_r == blk_c
    head_of_lane = lax.broadcasted_iota(jnp.int32, (CHUNK, gw), 1) // RWKV_HEAD_DIM
    bmm = functools.partial(jnp.einsum, preferred_element_type=F32)

    def block_diag(z):
        tiled = jnp.concatenate([z] * CORE_GROUP, axis=1)
        return jnp.where(same_head[None], tiled, jnp.zeros_like(tiled))

    def grouped(ref, g0):
        return jnp.stack([ref[b, g, :, gi * gw:(gi + 1) * gw]
                          for g in range(g0, g0 + span) for b in range(bsz) for gi in range(ngroups)], axis=0)

    def state_free_part(g0, outs):
        ar = grouped(ar_ref, g0)
        bk = grouped(bk_ref, g0)
        vv = grouped(v_ref, g0)
        rhs = jnp.concatenate([block_diag(bk[:, 0:CHUNK]), block_diag(bk[:, CHUNK:])], axis=1)
        s = bmm('bmk,bnk->bmn', ar, rhs)
        yield
        a_ab = jnp.where(strict[None], s[:, 0:CHUNK, 0:gw], 0.0)
        a_ak = jnp.where(strict[None], s[:, 0:CHUNK, gw:], 0.0).astype(BF16)
        a_rb = jnp.where(incl[None], s[:, CHUNK:, 0:gw], 0.0).astype(BF16)
        a_rk = jnp.where(incl[None], s[:, CHUNK:, gw:], 0.0).astype(BF16)
        p = eye[None] + a_ab
        qb = a_ab.astype(BF16)
        q = bmm('bij,bjk->bik', qb, block_diag(qb))
        yield
        for _ in range(4):
            qb = q.astype(BF16)
            both = bmm('bij,bjk->bik', jnp.concatenate([qb, p.astype(BF16)], axis=1), block_diag(qb))
            yield
            q = both[:, 0:CHUNK]
            p = p + both[:, CHUNK:]
        p = p + bmm('bij,bjk->bik', p.astype(BF16), block_diag(q.astype(BF16)))
        yield
        bdv = block_diag(vv)
        av = bmm('bij,bjk->bik', a_ak, bdv)
        tb = p.astype(BF16)
        for j in range(span):
            sl = slice(j * npar, (j + 1) * npar)
            outs[g0 + j].update(ar=ar[sl], bk=bk[sl], vv=vv[sl], a_rb=a_rb[sl], a_rk=a_rk[sl], tb=tb[sl],
                                bdv=bdv[sl], av=av[sl])
        yield

    def recurrent_part(g0, outs):
        for g in range(g0, g0 + span):
            yield from recurrent_chunk(g, outs[g])

    def recurrent_chunk(g, c):
        ht = ht_ref[...]
        arh = bmm('bmk,bnk->bmn', c['ar'], block_diag(ht.astype(BF16)))
        yield
        u = bmm('bij,bjk->bik', c['tb'], block_diag((arh[:, 0:CHUNK] + c['av']).astype(BF16)))
        yield
        ub = u.astype(BF16)
        y = (arh[:, CHUNK:] + bmm('bij,bjk->bik', c['a_rb'], block_diag(ub))
             + bmm('bij,bjk->bik', c['a_rk'], c['bdv']))
        cross = bmm('bsi,bsj->bij', jnp.concatenate([ub, c['vv']], axis=1), c['bk'])
        yield
        upd = sum(jnp.where(head_of_lane == h, cross[:, h * RWKV_HEAD_DIM:(h + 1) * RWKV_HEAD_DIM], 0.0)
                  for h in range(CORE_GROUP))
        for b in range(bsz):
            for gi in range(ngroups):
                n = b * ngroups + gi
                lanes = slice(gi * gw, (gi + 1) * gw)
                ht_ref[n] = wc_ref[b, g, 0:1, lanes] * (ht[n] + upd[n])
                y_ref[b, g * CHUNK:(g + 1) * CHUNK, lanes] = y[n]

    def run_together(*gens):
        live = list(gens)
        while live:
            for gen in list(live):
                if next(gen, StopIteration) is StopIteration:
                    live.remove(gen)

    span = CORE_SPAN
    chunks = [dict() for _ in range(nchunks)]
    run_together(state_free_part(0, chunks))
    for g0 in range(span, nchunks, span):
        run_together(state_free_part(g0, chunks), recurrent_part(g0 - span, chunks))
    run_together(recurrent_part(nchunks - span, chunks))


def _rwkv_core(ar, bk, v, wc):
    bsz, nc = ar.shape[:2]
    g = CORE_CHUNKS
    gw = CORE_GROUP * RWKV_HEAD_DIM
    blk = lambda rows: pl.BlockSpec((bsz, g, rows, RWKV_W), lambda i: (0, i, 0, 0))
    y = pl.pallas_call(
        _rwkv_core_kernel,
        grid=(nc // g,),
        in_specs=[blk(2 * CHUNK), blk(2 * CHUNK), blk(CHUNK), blk(SUBLANES)],
        out_specs=pl.BlockSpec((bsz, g * CHUNK, RWKV_W), lambda i: (0, i, 0)),
        out_shape=jax.ShapeDtypeStruct((bsz, nc * CHUNK, RWKV_W), F32),
        scratch_shapes=[pltpu.VMEM((bsz * (RWKV_W // gw), RWKV_HEAD_DIM, gw), F32)],
        compiler_params=_params(),
    )(ar, bk, v, wc)
    return y.reshape(bsz * nc * CHUNK, RWKV_W)


def _mixout_kernel(x_ref, c_ref, y_ref, gate_ref, bonus_ref, lng_ref, lnb_ref, ones_ref, w_ref, o_ref):
    inv_n = 1.0 / RWKV_HEAD_DIM
    y = y_ref[...]
    half = ones_ref.shape[0]

    def head_sum(z):
        zb = z.astype(BF16)
        return jnp.concatenate([jnp.dot(zb[:, c0:c0 + half], ones_ref[...], preferred_element_type=F32)
                                for c0 in range(0, z.shape[1], half)], axis=1)

    yc = y - head_sum(y) * inv_n
    var = head_sum(yc * yc) * inv_n
    yn = yc * lax.rsqrt(var + RWKV_GN_EPS) * lng_ref[...] + lnb_ref[...]
    yo = ((yn + bonus_ref[...]) * gate_ref[...]).astype(BF16)
    mix = (jnp.dot(c_ref[...], w_ref[0:CONV_W, :], preferred_element_type=F32)
           + jnp.dot(yo, w_ref[CONV_W:, :], preferred_element_type=F32))
    o_ref[...] = x_ref[...] + mix


def _mixout(x2, c, y, gate, bonus, lng, lnb, ones, w_out, tt):
    n, d = x2.shape
    row = lambda w: pl.BlockSpec((tt, w), lambda i: (i, 0))
    consts = [lng, lnb, ones, w_out]
    return pl.pallas_call(
        _mixout_kernel,
        grid=(n // tt,),
        in_specs=[row(d), row(CONV_W), row(RWKV_W), row(RWKV_W), row(RWKV_W)] + [_const_spec(c_.shape) for c_ in consts],
        out_specs=row(d),
        out_shape=jax.ShapeDtypeStruct((n, d), F32),
        compiler_params=_params(),
    )(x2, c, y, gate, bonus, *consts)


def _ffn_kernel(x_ref, halo_ref, g_ref, wup_ref, cw_ref, cb_ref, wdn_ref, o_ref, gs_ref, *, tiles_per_seq):
    tt = x_ref.shape[0]
    keep = (pl.program_id(0) % tiles_per_seq != 0).astype(F32)
    x = x_ref[...]
    xe = jnp.concatenate([halo_ref[...] * keep, x], axis=0)
    h = _rms_norm(xe, g_ref[...]).astype(BF16)
    u = jnp.dot(h, wup_ref[...], preferred_element_type=F32)
    gs_ref[...] = u[:, :D_FF]
    gate = cb_ref[...]
    for j in range(FFN_CONV_K):
        gate = gate + cw_ref[j:j + 1, :] * gs_ref[pl.ds(SUBLANES - (FFN_CONV_K - 1) + j, tt), :]
    act = (gate * _sigmoid(gate) * u[SUBLANES:, D_FF:]).astype(BF16)
    o_ref[...] = x + jnp.dot(act, wdn_ref[...], preferred_element_type=F32)


def _layer_spec(stacked, layer):
    tail = stacked.shape[1:]
    return pl.BlockSpec((None,) + tail, lambda *_: (layer,) + (0,) * len(tail), pipeline_mode=pl.Buffered(1))


def _ffn(x2, g, wup, cw, cb, wdn, layer, seq, tt):
    n, d = x2.shape
    tps = seq // tt
    hb = tt // SUBLANES
    kern = functools.partial(_ffn_kernel, tiles_per_seq=tps)
    consts = [g, wup, cw, cb, wdn]
    return pl.pallas_call(
        kern,
        grid=(n // tt,),
        in_specs=[pl.BlockSpec((tt, d), lambda i: (i, 0)),
                  pl.BlockSpec((SUBLANES, d), lambda i: (jnp.maximum(i * hb - 1, 0), 0))]
                 + [_layer_spec(c, layer) for c in consts],
        out_specs=pl.BlockSpec((tt, d), lambda i: (i, 0)),
        out_shape=jax.ShapeDtypeStruct((n, d), F32),
        scratch_shapes=[pltpu.VMEM((tt + SUBLANES, D_FF), F32)],
        compiler_params=_params(),
    )(x2, x2, *consts)


def _rope_table_kernel(freq_ref, pos_ref, cs_ref):
    half = ROT_DIM // 2
    pos = pos_ref[...]
    for f in range(half):
        ang = pos * freq_ref[f]
        cs_ref[f] = jnp.cos(ang)
        cs_ref[half + f] = jnp.sin(ang)


def _rope_table(inv_freq, pos_dense):
    rows, lanes = pos_dense.shape
    return pl.pallas_call(
        _rope_table_kernel,
        in_specs=[pl.BlockSpec(memory_space=pltpu.SMEM), pl.BlockSpec((rows, lanes), lambda: (0, 0))],
        out_specs=pl.BlockSpec((ROT_DIM, rows, lanes), lambda: (0, 0, 0)),
        out_shape=jax.ShapeDtypeStruct((ROT_DIM, rows, lanes), F32),
    )(inv_freq, pos_dense)


def _qkv_kernel(x_ref, cs_ref, g_ref, w_ref, b_ref, qg_ref, kg_ref, expand_ref, ones_ref,
                q_ref, k_ref, v_ref):
    qd = N_HEADS * HEAD_DIM
    kd = N_KV_HEADS * HEAD_DIM
    h = _rms_norm(x_ref[...], g_ref[...]).astype(BF16)
    qkv = jnp.dot(h, w_ref[...], preferred_element_type=F32) + b_ref[...]

    lanes = expand_ref.shape[1] // 3
    cs_hi, cs_lo, _ = _split3(cs_ref[...])
    tab = (jnp.dot(cs_hi, expand_ref[...], preferred_element_type=F32)
           + jnp.dot(cs_lo, expand_ref[...], preferred_element_type=F32))
    dim = lax.broadcasted_iota(jnp.int32, (1, lanes), 1) % HEAD_DIM
    cos = tab[:, :lanes] + (dim >= ROT_DIM).astype(F32)
    sin_a = tab[:, lanes:2 * lanes]
    sin_b = tab[:, 2 * lanes:]
    half = ROT_DIM // 2

    def norm_rope(z, gain, scale):
        ss = jnp.dot((z * z).astype(BF16), ones_ref[...], preferred_element_type=F32)
        z = z * lax.rsqrt(ss * (1.0 / HEAD_DIM) + NORM_EPS) * gain
        outs = []
        for c0 in range(0, z.shape[1], lanes):
            zz = z[:, c0:c0 + lanes]
            rot = (zz * cos + pltpu.roll(zz, lanes - half, 1) * sin_a + pltpu.roll(zz, half, 1) * sin_b)
            outs.append(rot * scale)
        return outs

    blk = 4 * HEAD_DIM
    for c0 in range(0, qd, blk):
        outs = norm_rope(qkv[:, c0:c0 + blk], qg_ref[...], HEAD_DIM ** -0.5)
        for j, o in enumerate(outs):
            hh = c0 // HEAD_DIM + 2 * j
            q_ref[0, hh] = o[:, :HEAD_DIM].astype(BF16)
            q_ref[0, hh + 1] = o[:, HEAD_DIM:].astype(BF16)
    outs = norm_rope(qkv[:, qd:qd + kd], kg_ref[...], 1.0)
    for j, o in enumerate(outs):
        k_ref[0, 2 * j] = o[:, :HEAD_DIM].astype(BF16)
        k_ref[0, 2 * j + 1] = o[:, HEAD_DIM:].astype(BF16)
    vv = qkv[:, qd + kd:]
    for j in range(N_KV_HEADS):
        v_ref[0, j] = vv[:, j * HEAD_DIM:(j + 1) * HEAD_DIM].astype(BF16)


def _qkv(x2, cs, g, w, b, qg, kg, expand, ones, bsz, seq, tt):
    n, d = x2.shape
    tps = seq // tt
    consts = [g, w, b, qg, kg, expand, ones]
    hm = lambda nh: pl.BlockSpec((1, nh, tt, HEAD_DIM), lambda i: (i // tps, 0, i % tps, 0))
    return pl.pallas_call(
        _qkv_kernel,
        grid=(n // tt,),
        in_specs=[pl.BlockSpec((tt, d), lambda i: (i, 0)), pl.BlockSpec((tt, ROT_DIM), lambda i: (i, 0))]
                 + [_const_spec(c.shape) for c in consts],
        out_specs=[hm(N_HEADS), hm(N_KV_HEADS), hm(N_KV_HEADS)],
        out_shape=[jax.ShapeDtypeStruct((bsz, N_HEADS, seq, HEAD_DIM), BF16),
                   jax.ShapeDtypeStruct((bsz, N_KV_HEADS, seq, HEAD_DIM), BF16),
                   jax.ShapeDtypeStruct((bsz, N_KV_HEADS, seq, HEAD_DIM), BF16)],
        compiler_params=_params(),
    )(x2, cs, *consts)


def _attn_kernel(sink_ref, q_ref, kc_ref, kp_ref, vc_ref, vp_ref, x_ref, w_ref, b_ref, out_ref, *, q_blocks):
    i = pl.program_id(1)
    blk = WINDOW
    rows = GROUP * blk
    t = lax.broadcasted_iota(jnp.int32, (rows, 2 * blk), 0) % blk
    c = lax.broadcasted_iota(jnp.int32, (rows, 2 * blk), 1)
    d = c - t
    band = (d >= 1) & (d <= blk)
    head_of_row = lax.broadcasted_iota(jnp.int32, (rows, 1), 0) // blk
    blocks = []
    for qb in range(q_blocks):
        if qb == 0:
            valid = band & ((c >= blk) | (i > 0))
        else:
            valid = band
        outs = []
        for g in range(N_KV_HEADS):
            q = q_ref[0, g * GROUP:(g + 1) * GROUP, qb * blk:(qb + 1) * blk, :].reshape(rows, HEAD_DIM)
            if qb == 0:
                kprev, vprev = kp_ref[0, g], vp_ref[0, g]
            else:
                kprev = kc_ref[0, g, (qb - 1) * blk:qb * blk, :]
                vprev = vc_ref[0, g, (qb - 1) * blk:qb * blk, :]
            k = jnp.concatenate([kprev, kc_ref[0, g, qb * blk:(qb + 1) * blk, :]], axis=0)
            v = jnp.concatenate([vprev, vc_ref[0, g, qb * blk:(qb + 1) * blk, :]], axis=0)
            s = lax.dot_general(q, k, NT_DIMS, preferred_element_type=F32)
            s = jnp.where(valid, s, -jnp.inf)
            sink = jnp.zeros((rows, 1), F32)
            for j in range(GROUP):
                sink = jnp.where(head_of_row == j, sink_ref[g * GROUP + j], sink)
            m = jnp.maximum(jnp.max(s, axis=-1, keepdims=True), sink)
            p = jnp.exp(s - m)
            den = jnp.sum(p, axis=-1, keepdims=True) + jnp.exp(sink - m)
            o = jnp.dot(p.astype(BF16), v, preferred_element_type=F32) / den
            outs.extend(o[j * blk:(j + 1) * blk] for j in range(GROUP))
        blocks.append(jnp.concatenate(outs, axis=1).astype(BF16))
    o_all = jnp.concatenate(blocks, axis=0)
    out_ref[...] = x_ref[...] + jnp.dot(o_all, w_ref[...], preferred_element_type=F32) + b_ref[...]


def _attention(sinks, q, k, v, x2, w_o, b_o, q_blocks):
    bsz, _, seq, _ = q.shape
    d = x2.shape[1]
    tq = q_blocks * WINDOW
    steps = seq // tq
    kern = functools.partial(_attn_kernel, q_blocks=q_blocks)
    cur = lambda nh: pl.BlockSpec((1, nh, tq, HEAD_DIM), lambda b, i: (b, 0, i, 0))
    prev = pl.BlockSpec((1, N_KV_HEADS, WINDOW, HEAD_DIM), lambda b, i: (b, 0, jnp.maximum(i * q_blocks - 1, 0), 0))
    rows = pl.BlockSpec((tq, d), lambda b, i: (b * steps + i, 0))
    return pl.pallas_call(
        kern,
        grid=(bsz, steps),
        in_specs=[pl.BlockSpec(memory_space=pltpu.SMEM), cur(N_HEADS), cur(N_KV_HEADS), prev, cur(N_KV_HEADS), prev,
                  rows, _const_spec(w_o.shape), _const_spec(b_o.shape)],
        out_specs=rows,
        out_shape=jax.ShapeDtypeStruct((bsz * seq, d), F32),
        compiler_params=_params(2),
    )(sinks, q, k, k, v, v, x2, w_o, b_o)


def _block_ones(width, group):
    idx = np.arange(width) // group
    return jnp.asarray(idx[:, None] == idx[None, :], BF16)


def _chunk_tril(rows):
    idx = np.arange(rows)
    same = (idx[:, None] // CHUNK) == (idx[None, :] // CHUNK)
    return jnp.asarray(same & (idx[None, :] <= idx[:, None]), BF16)


def _rope_expand(lanes=128):
    half = ROT_DIM // 2
    dim = np.arange(lanes) % HEAD_DIM
    f = np.arange(half)[:, None]
    hit = (dim[None, :] % half == f)
    zero = np.zeros((half, lanes))
    cos_rows = np.concatenate([hit & (dim < ROT_DIM)[None, :], zero, zero], axis=1)
    sin_rows = np.concatenate([zero, -1.0 * (hit & (dim < half)[None, :]),
                               hit & ((dim >= half) & (dim < ROT_DIM))[None, :]], axis=1)
    return jnp.asarray(np.concatenate([cos_rows, sin_rows], axis=0), BF16)


def _row(v):
    return v.reshape(1, -1)


def kernel(x, positions, ab_norm_g, ab_w_in, conv_in_b, conv_dw_w, conv_dw_b, conv_ln_g, conv_ln_b, rwkv_mu, rwkv_w0, rwkv_w2, rwkv_a0, rwkv_a2, rwkv_g2, rwkv_k_k, rwkv_k_a, rwkv_r_k, rwkv_ln_g, rwkv_ln_b, ab_w_out, attn_norm_g, attn_w_qkv, attn_b_qkv, attn_q_norm_g, attn_k_norm_g, attn_sinks, attn_w_o, attn_b_o, ffn_norm_g, ffn_w_up, ffn_conv_w, ffn_conv_b, ffn_w_down):
    bsz, seq, d = x.shape
    depth = ffn_norm_g.shape[0]
    n = bsz * seq
    tt = 512
    ffn_tt = 512
    x2 = x.reshape(n, d)
    ones_rwkv = _block_ones(RWKV_W, RWKV_HEAD_DIM)
    ones_quad = _block_ones(4 * HEAD_DIM, HEAD_DIM)
    tri = _chunk_tril(256)
    rope_expand = _rope_expand()
    ffn_g = ffn_norm_g[:, None, :]
    ffn_cb = ffn_conv_b[:, None, :]
    ffn_up = ffn_w_up.astype(BF16)
    ffn_down = ffn_w_down.astype(BF16)

    for layer in range(depth):
        i = layer // 2
        if layer % 2 == 0:
            zeros = jnp.zeros((LORA_DECAY, RWKV_W), F32)
            w2a2 = jnp.concatenate([jnp.concatenate([rwkv_w2[i], zeros], axis=1),
                                    jnp.concatenate([zeros, rwkv_a2[i]], axis=1)], axis=0).astype(BF16)
            c, ar, bk, vv, wc, gate, bonus = _front(
                x2, _row(ab_norm_g[i]), ab_w_in[i].astype(BF16),
                _row(conv_in_b[i]), conv_dw_w[i], _row(conv_dw_b[i]), _row(conv_ln_g[i]), _row(conv_ln_b[i]),
                _row(rwkv_mu[i]), _row(rwkv_w0[i]), w2a2, _row(rwkv_a0[i]), rwkv_g2[i].astype(BF16),
                _row(rwkv_k_k[i]), _row(rwkv_k_a[i]), _row(rwkv_r_k[i]), ones_rwkv, tri, bsz, seq, tt)
            y = _rwkv_core(ar, bk, vv, wc)
            x2 = _mixout(x2, c, y, gate, bonus, _row(rwkv_ln_g[i]), _row(rwkv_ln_b[i]), ones_quad,
                         ab_w_out[i].astype(BF16), tt)
        else:
            half = ROT_DIM // 2
            inv_freq = ROPE_THETA ** (-(jnp.arange(half, dtype=F32) * 2.0) / ROT_DIM)
            lanes = 128
            cs = _rope_table(inv_freq, positions.astype(F32).reshape(n // lanes, lanes))
            cs = cs.reshape(ROT_DIM, n).T
            q, k, v = _qkv(x2, cs, _row(attn_norm_g[i]), attn_w_qkv[i].astype(BF16), _row(attn_b_qkv[i]),
                           _row(jnp.tile(attn_q_norm_g[i], 4)), _row(jnp.tile(attn_k_norm_g[i], 4)),
                           rope_expand, ones_quad, bsz, seq, tt)
            x2 = _attention(attn_sinks[i], q, k, v, x2, attn_w_o[i].astype(BF16), _row(attn_b_o[i]), 4)
        x2 = _ffn(x2, ffn_g, ffn_up, ffn_conv_w, ffn_cb, ffn_down, layer, seq, ffn_tt)
    return x2.reshape(bsz, seq, d)
```

```python
import functools

import jax
import jax.numpy as jnp
import numpy as np
from jax import lax
from jax.experimental import pallas as pl
from jax.experimental.pallas import tpu as pltpu

F32 = jnp.float32
BF16 = jnp.bfloat16

CONV_W = 512
CONV_K = 31
CONV_LN_EPS = 1e-5
RWKV_HEADS = 8
RWKV_HEAD_DIM = 64
RWKV_W = RWKV_HEADS * RWKV_HEAD_DIM
LORA_DECAY = 64
LORA_ICLR = 64
LORA_GATE = 128
RWKV_GN_EPS = RWKV_HEAD_DIM * 1e-5
RWKV_IN = 3 * RWKV_W + LORA_DECAY + LORA_ICLR + LORA_GATE
HEAD_DIM = 64
N_HEADS = 16
N_KV_HEADS = 4
GROUP = N_HEADS // N_KV_HEADS
WINDOW = 128
ROT_DIM = 16
ROPE_THETA = 500000.0
D_FF = 2816
FFN_CONV_K = 3
NORM_EPS = 1e-6
LOG2_E = 1.4426950408889634

CHUNK = 64
CORE_CHUNKS = 16
CORE_SPAN = 2
CORE_GROUP = 4
SUBLANES = 8
CONV_HALO = 32
V7X_VMEM_LIMIT = 56 * 1024 * 1024

NT_DIMS = (((1,), (1,)), ((), ()))
TN_DIMS = (((0,), (0,)), ((), ()))


def _params(n_axes=1):
    return pltpu.CompilerParams(dimension_semantics=("arbitrary",) * n_axes,
                                vmem_limit_bytes=V7X_VMEM_LIMIT)


def _const_spec(shape):
    nd = len(shape)
    return pl.BlockSpec(shape, lambda *_: (0,) * nd, pipeline_mode=pl.Buffered(1))


def _rms_norm(x, g):
    return x * lax.rsqrt(jnp.mean(x * x, axis=-1, keepdims=True) + NORM_EPS) * g


def _sigmoid(z):
    return 1.0 / (1.0 + jnp.exp(-z))


def _split3(z):
    hi = z.astype(BF16)
    r1 = z - hi.astype(F32)
    mid = r1.astype(BF16)
    lo = (r1 - mid.astype(F32)).astype(BF16)
    return hi, mid, lo


def _conv_shift_copies(ext_ref, tt):
    span = tt + CONV_HALO - SUBLANES
    for s in range(1, SUBLANES):
        ext_ref[s, 0:span, :] = ext_ref[0, pl.ds(s, span), :]


def _conv_taps(ext_ref, dww_ref, dwb_ref, lng_ref, lnb_ref, c_ref, tt, row_block):
    first = CONV_HALO - (CONV_K - 1)
    for r0 in range(0, tt, row_block):
        acc = jnp.broadcast_to(dwb_ref[...], (row_block, CONV_W))
        for s in range(SUBLANES):
            taps = [j for j in range(CONV_K) if (first + j) % SUBLANES == s]
            lo = first + taps[0] - s
            hi = first + taps[-1] - s
            win = ext_ref[s, pl.ds(r0 + lo, row_block + hi - lo), :]
            for j in taps:
                a = first + j - s - lo
                acc = acc + dww_ref[j:j + 1, :] * win[a:a + row_block]
        mu = jnp.mean(acc, axis=-1, keepdims=True)
        xc = acc - mu
        var = jnp.mean(xc * xc, axis=-1, keepdims=True)
        z = xc * lax.rsqrt(var + CONV_LN_EPS) * lng_ref[...] + lnb_ref[...]
        c_ref[r0:r0 + row_block, :] = (z * _sigmoid(z)).astype(c_ref.dtype)


def _convmod(pc, inb, dww, dwb, lng, lnb, seq, tt):
    n = pc.shape[0]
    tps = seq // tt
    hb = tt // CONV_HALO
    kern = functools.partial(_convmod_kernel, tiles_per_seq=tps, row_block=64)
    return pl.pallas_call(
        kern,
        grid=(n // tt,),
        in_specs=[pl.BlockSpec((tt, 2 * CONV_W), lambda i: (i, 0)),
                  pl.BlockSpec((CONV_HALO, 2 * CONV_W), lambda i: (jnp.maximum(i * hb - 1, 0), 0)),
                  _const_spec(inb.shape), _const_spec(dww.shape), _const_spec(dwb.shape),
                  _const_spec(lng.shape), _const_spec(lnb.shape)],
        out_specs=pl.BlockSpec((tt, CONV_W), lambda i: (i, 0)),
        out_shape=jax.ShapeDtypeStruct((n, CONV_W), BF16),
        scratch_shapes=[pltpu.VMEM((SUBLANES, tt + CONV_HALO, CONV_W), F32)],
        compiler_params=_params(),
    )(pc, pc, inb, dww, dwb, lng, lnb)


def _rwkv_prep_body(ext_ref, mu_ref, w0_ref, w2a2_ref, a0_ref, g2_ref, kk_ref, ka_ref, rk_ref, ones_ref, tri_ref,
                    ar_ref, bk_ref, v_ref, wc_ref, gate_ref, bonus_ref, tt):
    rw = ext_ref[SUBLANES:, :]
    rw = rw + (ext_ref[pl.ds(SUBLANES - 1, tt), :] - rw) * mu_ref[...]

    r = rw[:, 0:RWKV_W]
    k = rw[:, RWKV_W:2 * RWKV_W]
    v = rw[:, 2 * RWKV_W:3 * RWKV_W]
    wa = rw[:, 3 * RWKV_W:3 * RWKV_W + LORA_DECAY + LORA_ICLR]
    gd = rw[:, 3 * RWKV_W + LORA_DECAY + LORA_ICLR:]

    lane = lax.broadcasted_iota(jnp.int32, wa.shape, 1)
    z = jnp.where(lane < LORA_DECAY, jnp.tanh(wa), wa)
    proj = jnp.dot(z.astype(BF16), w2a2_ref[...], preferred_element_type=F32)
    zw = -(w0_ref[...] + proj[:, :RWKV_W])
    softplus = jnp.maximum(zw, 0.0) + jnp.log(1.0 + jnp.exp(-jnp.abs(zw)))
    lw = -jnp.exp(-softplus - 0.5)
    a = _sigmoid(a0_ref[...] + proj[:, RWKV_W:])
    gate_ref[...] = jnp.dot(_sigmoid(gd).astype(BF16), g2_ref[...], preferred_element_type=F32)

    kk = k * kk_ref[...]
    ss = jnp.dot((kk * kk).astype(BF16), ones_ref[...], preferred_element_type=F32)
    kk = kk / jnp.maximum(jnp.sqrt(ss), 1e-12)
    k = k * (1.0 + (a - 1.0) * ka_ref[...])
    rk = jnp.dot((r * k * rk_ref[...]).astype(BF16), ones_ref[...], preferred_element_type=F32)
    bonus_ref[...] = rk * v

    half = tri_ref.shape[0]
    cums = []
    for r0 in range(0, tt, half):
        parts = _split3(lw[r0:r0 + half])
        cums.append(sum(jnp.dot(tri_ref[...], p, preferred_element_type=F32) for p in parts))
    cum = jnp.concatenate(cums, axis=0)
    e_cum = jnp.exp(cum)
    e_inv = jnp.exp(-cum)
    rt = r * e_cum
    at = -kk * jnp.exp(cum - lw)
    kt = k * e_inv
    bt = kk * a * e_inv

    nch = tt // CHUNK
    for c in range(nch):
        rows = slice(c * CHUNK, (c + 1) * CHUNK)
        ar_ref[0, c, 0:CHUNK, :] = at[rows].astype(BF16)
        ar_ref[0, c, CHUNK:, :] = rt[rows].astype(BF16)
        bk_ref[0, c, 0:CHUNK, :] = bt[rows].astype(BF16)
        bk_ref[0, c, CHUNK:, :] = kt[rows].astype(BF16)
        v_ref[0, c, :, :] = v[rows].astype(BF16)
        wc_ref[0, c, :, :] = jnp.broadcast_to(e_cum[(c + 1) * CHUNK - 1:(c + 1) * CHUNK], (SUBLANES, RWKV_W))


def _front_kernel(x0_ref, xn_ref, ng_ref, win_ref, inb_ref, dww_ref, dwb_ref, lng_ref, lnb_ref,
                  mu_ref, w0_ref, w2a2_ref, a0_ref, g2_ref, kk_ref, ka_ref, rk_ref, ones_ref, tri_ref,
                  c_ref, ar_ref, bk_ref, v_ref, wc_ref, gate_ref, bonus_ref, ext_ref, prc_ref,
                  *, tiles_per_seq, row_block):
    i = pl.program_id(0)
    tt = xn_ref.shape[0]

    def project(x_ref):
        h = _rms_norm(x_ref[...], ng_ref[...])
        return jnp.dot(h.astype(BF16), win_ref[...], preferred_element_type=F32)

    def make_current(p, conv_hist, shift_hist):
        z = p[:, :2 * CONV_W] + inb_ref[...]
        ext_ref[0, 0:CONV_HALO, :] = conv_hist
        ext_ref[0, CONV_HALO:, :] = z[:, :CONV_W] * _sigmoid(z[:, CONV_W:])
        _conv_shift_copies(ext_ref, tt)
        prc_ref[0:SUBLANES, :] = shift_hist
        prc_ref[SUBLANES:, :] = p[:, 2 * CONV_W:]

    @pl.when(i == 0)
    def _():
        make_current(project(x0_ref), jnp.zeros((CONV_HALO, CONV_W), F32), jnp.zeros((SUBLANES, RWKV_IN), F32))

    p_next = project(xn_ref)
    _conv_taps(ext_ref, dww_ref, dwb_ref, lng_ref, lnb_ref, c_ref, tt, row_block)
    _rwkv_prep_body(prc_ref, mu_ref, w0_ref, w2a2_ref, a0_ref, g2_ref, kk_ref, ka_ref, rk_ref, ones_ref, tri_ref,
                    ar_ref, bk_ref, v_ref, wc_ref, gate_ref, bonus_ref, tt)
    same_seq = (i + 1) % tiles_per_seq != 0
    make_current(p_next,
                 jnp.where(same_seq, ext_ref[0, tt:tt + CONV_HALO, :], 0.0),
                 jnp.where(same_seq, prc_ref[tt:tt + SUBLANES, :], 0.0))


def _front(x2, ng, w_in, inb, dww, dwb, lng, lnb, mu, w0, w2a2, a0, g2, k_k, k_a, r_k, ones, tri, bsz, seq, tt):
    n, d = x2.shape
    tps = seq // tt
    steps = n // tt
    nch = tt // CHUNK
    nc = seq // CHUNK
    kern = functools.partial(_front_kernel, tiles_per_seq=tps, row_block=64)
    hm = lambda rows: pl.BlockSpec((1, nch, rows, RWKV_W), lambda i: (i // tps, i % tps, 0, 0))
    hm_shape = lambda rows, dt: jax.ShapeDtypeStruct((bsz, nc, rows, RWKV_W), dt)
    consts = [ng, w_in, inb, dww, dwb, lng, lnb, mu, w0, w2a2, a0, g2, k_k, k_a, r_k, ones, tri]
    row = lambda w: pl.BlockSpec((tt, w), lambda i: (i, 0))
    return pl.pallas_call(
        kern,
        grid=(steps,),
        in_specs=[pl.BlockSpec((tt, d), lambda i: (0, 0)),
                  pl.BlockSpec((tt, d), lambda i: (jnp.minimum(i + 1, steps - 1), 0))]
                 + [_const_spec(c.shape) for c in consts],
        out_specs=[row(CONV_W), hm(2 * CHUNK), hm(2 * CHUNK), hm(CHUNK), hm(SUBLANES), row(RWKV_W), row(RWKV_W)],
        out_shape=[jax.ShapeDtypeStruct((n, CONV_W), BF16),
                   hm_shape(2 * CHUNK, BF16), hm_shape(2 * CHUNK, BF16), hm_shape(CHUNK, BF16), hm_shape(SUBLANES, F32),
                   jax.ShapeDtypeStruct((n, RWKV_W), F32), jax.ShapeDtypeStruct((n, RWKV_W), F32)],
        scratch_shapes=[pltpu.VMEM((SUBLANES, tt + CONV_HALO, CONV_W), F32),
                        pltpu.VMEM((tt + SUBLANES, RWKV_IN), F32)],
        compiler_params=_params(),
    )(x2, x2, *consts)


def _rwkv_core_kernel(ar_ref, bk_ref, v_ref, wc_ref, y_ref, ht_ref):
    @pl.when(pl.program_id(0) == 0)
    def _():
        ht_ref[...] = jnp.zeros_like(ht_ref)

    bsz, nchunks = ar_ref.shape[0], ar_ref.shape[1]
    gw = CORE_GROUP * RWKV_HEAD_DIM
    ngroups = RWKV_W // gw
    npar = bsz * ngroups
    row = lax.broadcasted_iota(jnp.int32, (CHUNK, gw), 0)
    col = lax.broadcasted_iota(jnp.int32, (CHUNK, gw), 1) % RWKV_HEAD_DIM
    strict = col < row
    incl = col <= row
    eye = (col == row).astype(F32)
    blk_r = lax.broadcasted_iota(jnp.int32, (gw, gw), 0) // RWKV_HEAD_DIM
    blk_c = lax.broadcasted_iota(jnp.int32, (gw, gw), 1) // RWKV_HEAD_DIM
    same_head = blk_r == blk_c
    head_of_lane = lax.broadcasted_iota(jnp.int32, (CHUNK, gw), 1) // RWKV_HEAD_DIM
    bmm = functools.partial(jnp.einsum, preferred_element_type=F32)

    def block_diag(z):
        tiled = jnp.concatenate([z] * CORE_GROUP, axis=1)
        return jnp.where(same_head[None], tiled, jnp.zeros_like(tiled))

    def grouped(ref, g0):
        return jnp.stack([ref[b, g, :, gi * gw:(gi + 1) * gw]
                          for g in range(g0, g0 + span) for b in range(bsz) for gi in range(ngroups)], axis=0)

    def state_free_part(g0, outs):
        ar = grouped(ar_ref, g0)
        bk = grouped(bk_ref, g0)
        vv = grouped(v_ref, g0)
        rhs = jnp.concatenate([block_diag(bk[:, 0:CHUNK]), block_diag(bk[:, CHUNK:])], axis=1)
        s = bmm('bmk,bnk->bmn', ar, rhs)
        yield
        a_ab = jnp.where(strict[None], s[:, 0:CHUNK, 0:gw], 0.0)
        a_ak = jnp.where(strict[None], s[:, 0:CHUNK, gw:], 0.0).astype(BF16)
        a_rb = jnp.where(incl[None], s[:, CHUNK:, 0:gw], 0.0).astype(BF16)
        a_rk = jnp.where(incl[None], s[:, CHUNK:, gw:], 0.0).astype(BF16)
        p = eye[None] + a_ab
        qb = a_ab.astype(BF16)
        q = bmm('bij,bjk->bik', qb, block_diag(qb))
        yield
        for _ in range(4):
            qb = q.astype(BF16)
            both = bmm('bij,bjk->bik', jnp.concatenate([qb, p.astype(BF16)], axis=1), block_diag(qb))
            yield
            q = both[:, 0:CHUNK]
            p = p + both[:, CHUNK:]
        p = p + bmm('bij,bjk->bik', p.astype(BF16), block_diag(q.astype(BF16)))
        yield
        bdv = block_diag(vv)
        av = bmm('bij,bjk->bik', a_ak, bdv)
        tb = p.astype(BF16)
        for j in range(span):
            sl = slice(j * npar, (j + 1) * npar)
            outs[g0 + j].update(ar=ar[sl], bk=bk[sl], vv=vv[sl], a_rb=a_rb[sl], a_rk=a_rk[sl], tb=tb[sl],
                                bdv=bdv[sl], av=av[sl])
        yield

    def recurrent_part(g0, outs):
        for g in range(g0, g0 + span):
            yield from recurrent_chunk(g, outs[g])

    def recurrent_chunk(g, c):
        ht = ht_ref[...]
        arh = bmm('bmk,bnk->bmn', c['ar'], block_diag(ht.astype(BF16)))
        yield
        u = bmm('bij,bjk->bik', c['tb'], block_diag((arh[:, 0:CHUNK] + c['av']).astype(BF16)))
        yield
        ub = u.astype(BF16)
        y = (arh[:, CHUNK:] + bmm('bij,bjk->bik', c['a_rb'], block_diag(ub))
             + bmm('bij,bjk->bik', c['a_rk'], c['bdv']))
        cross = bmm('bsi,bsj->bij', jnp.concatenate([ub, c['vv']], axis=1), c['bk'])
        yield
        upd = sum(jnp.where(head_of_lane == h, cross[:, h * RWKV_HEAD_DIM:(h + 1) * RWKV_HEAD_DIM], 0.0)
                  for h in range(CORE_GROUP))
        for b in range(bsz):
            for gi in range(ngroups):
                n = b * ngroups + gi
                lanes = slice(gi * gw, (gi + 1) * gw)
                ht_ref[n] = wc_ref[b, g, 0:1, lanes] * (ht[n] + upd[n])
                y_ref[b, g * CHUNK:(g + 1) * CHUNK, lanes] = y[n]

    def run_together(*gens):
        live = list(gens)
        while live:
            for gen in list(live):
                if next(gen, StopIteration) is StopIteration:
                    live.remove(gen)

    span = CORE_SPAN
    chunks = [dict() for _ in range(nchunks)]
    run_together(state_free_part(0, chunks))
    for g0 in range(span, nchunks, span):
        run_together(state_free_part(g0, chunks), recurrent_part(g0 - span, chunks))
    run_together(recurrent_part(nchunks - span, chunks))


def _rwkv_core(ar, bk, v, wc):
    bsz, nc = ar.shape[:2]
    g = CORE_CHUNKS
    gw = CORE_GROUP * RWKV_HEAD_DIM
    blk = lambda rows: pl.BlockSpec((bsz, g, rows, RWKV_W), lambda i: (0, i, 0, 0))
    y = pl.pallas_call(
        _rwkv_core_kernel,
        grid=(nc // g,),
        in_specs=[blk(2 * CHUNK), blk(2 * CHUNK), blk(CHUNK), blk(SUBLANES)],
        out_specs=pl.BlockSpec((bsz, g * CHUNK, RWKV_W), lambda i: (0, i, 0)),
        out_shape=jax.ShapeDtypeStruct((bsz, nc * CHUNK, RWKV_W), F32),
        scratch_shapes=[pltpu.VMEM((bsz * (RWKV_W // gw), RWKV_HEAD_DIM, gw), F32)],
        compiler_params=_params(),
    )(ar, bk, v, wc)
    return y.reshape(bsz * nc * CHUNK, RWKV_W)


def _mix_ffn_kernel(x_ref, c_ref, y_ref, gate_ref, bonus_ref, lng_ref, lnb_ref, ones_ref, w_ref,
                    g_ref, wup_ref, cw_ref, cb_ref, wdn_ref, o_ref, hist_ref, gs_ref, *, tiles_per_seq):
    i = pl.program_id(0)
    tt = x_ref.shape[0]

    @pl.when(i == 0)
    def _():
        hist_ref[...] = jnp.zeros_like(hist_ref)

    inv_n = 1.0 / RWKV_HEAD_DIM
    y = y_ref[...]
    half = ones_ref.shape[0]

    def head_sum(z):
        zb = z.astype(BF16)
        return jnp.concatenate([jnp.dot(zb[:, c0:c0 + half], ones_ref[...], preferred_element_type=F32)
                                for c0 in range(0, z.shape[1], half)], axis=1)

    yc = y - head_sum(y) * inv_n
    var = head_sum(yc * yc) * inv_n
    yn = yc * lax.rsqrt(var + RWKV_GN_EPS) * lng_ref[...] + lnb_ref[...]
    yo = ((yn + bonus_ref[...]) * gate_ref[...]).astype(BF16)
    mix = (jnp.dot(c_ref[...], w_ref[0:CONV_W, :], preferred_element_type=F32)
           + jnp.dot(yo, w_ref[CONV_W:, :], preferred_element_type=F32))
    x1 = x_ref[...] + mix
    hist = jnp.where(i % tiles_per_seq != 0, hist_ref[...], 0.0)
    o_ref[...] = _ffn_tile(x1, hist, g_ref, wup_ref, cw_ref, cb_ref, wdn_ref, gs_ref)
    hist_ref[...] = x1[tt - SUBLANES:, :]


def _mix_ffn(x2, c, y, gate, bonus, lng, lnb, ones, w_out, g, wup, cw, cb, wdn, layer, seq, tt):
    n, d = x2.shape
    row = lambda w: pl.BlockSpec((tt, w), lambda i: (i, 0))
    consts = [lng, lnb, ones, w_out]
    ffn_consts = [g, wup, cw, cb, wdn]
    kern = functools.partial(_mix_ffn_kernel, tiles_per_seq=seq // tt)
    return pl.pallas_call(
        kern,
        grid=(n // tt,),
        in_specs=[row(d), row(CONV_W), row(RWKV_W), row(RWKV_W), row(RWKV_W)]
                 + [_const_spec(c_.shape) for c_ in consts] + [_layer_spec(c_, layer) for c_ in ffn_consts],
        out_specs=row(d),
        out_shape=jax.ShapeDtypeStruct((n, d), F32),
        scratch_shapes=[pltpu.VMEM((SUBLANES, d), F32), pltpu.VMEM((tt + SUBLANES, D_FF), F32)],
        compiler_params=_params(),
    )(x2, c, y, gate, bonus, *consts, *ffn_consts)


def _ffn_tile(x, hist, g_ref, wup_ref, cw_ref, cb_ref, wdn_ref, gs_ref):
    tt = x.shape[0]
    xe = jnp.concatenate([hist, x], axis=0)
    h = _rms_norm(xe, g_ref[...]).astype(BF16)
    u = jnp.dot(h, wup_ref[...], preferred_element_type=F32)
    gs_ref[...] = u[:, :D_FF]
    gate = cb_ref[...]
    for j in range(FFN_CONV_K):
        gate = gate + cw_ref[j:j + 1, :] * gs_ref[pl.ds(SUBLANES - (FFN_CONV_K - 1) + j, tt), :]
    act = (gate * _sigmoid(gate) * u[SUBLANES:, D_FF:]).astype(BF16)
    return x + jnp.dot(act, wdn_ref[...], preferred_element_type=F32)


def _ffn_kernel(x_ref, halo_ref, g_ref, wup_ref, cw_ref, cb_ref, wdn_ref, o_ref, gs_ref, *, tiles_per_seq):
    keep = (pl.program_id(0) % tiles_per_seq != 0).astype(F32)
    o_ref[...] = _ffn_tile(x_ref[...], halo_ref[...] * keep, g_ref, wup_ref, cw_ref, cb_ref, wdn_ref, gs_ref)


def _layer_spec(stacked, layer):
    tail = stacked.shape[1:]
    return pl.BlockSpec((None,) + tail, lambda *_: (layer,) + (0,) * len(tail), pipeline_mode=pl.Buffered(1))


def _ffn(x2, g, wup, cw, cb, wdn, layer, seq, tt):
    n, d = x2.shape
    tps = seq // tt
    hb = tt // SUBLANES
    kern = functools.partial(_ffn_kernel, tiles_per_seq=tps)
    consts = [g, wup, cw, cb, wdn]
    return pl.pallas_call(
        kern,
        grid=(n // tt,),
        in_specs=[pl.BlockSpec((tt, d), lambda i: (i, 0)),
                  pl.BlockSpec((SUBLANES, d), lambda i: (jnp.maximum(i * hb - 1, 0), 0))]
                 + [_layer_spec(c, layer) for c in consts],
        out_specs=pl.BlockSpec((tt, d), lambda i: (i, 0)),
        out_shape=jax.ShapeDtypeStruct((n, d), F32),
        scratch_shapes=[pltpu.VMEM((tt + SUBLANES, D_FF), F32)],
        compiler_params=_params(),
    )(x2, x2, *consts)


def _rope_table_kernel(freq_ref, pos_ref, cs_ref):
    half = ROT_DIM // 2
    pos = pos_ref[...]
    for f in range(half):
        ang = pos * freq_ref[f]
        cs_ref[f] = jnp.cos(ang)
        cs_ref[half + f] = jnp.sin(ang)


def _rope_table(inv_freq, pos_dense):
    rows, lanes = pos_dense.shape
    return pl.pallas_call(
        _rope_table_kernel,
        in_specs=[pl.BlockSpec(memory_space=pltpu.SMEM), pl.BlockSpec((rows, lanes), lambda: (0, 0))],
        out_specs=pl.BlockSpec((ROT_DIM, rows, lanes), lambda: (0, 0, 0)),
        out_shape=jax.ShapeDtypeStruct((ROT_DIM, rows, lanes), F32),
    )(inv_freq, pos_dense)


def _qkv_kernel(x_ref, cs_ref, g_ref, w_ref, b_ref, qg_ref, kg_ref, expand_ref, ones_ref,
                q_ref, k_ref, v_ref):
    qd = N_HEADS * HEAD_DIM
    kd = N_KV_HEADS * HEAD_DIM
    h = _rms_norm(x_ref[...], g_ref[...]).astype(BF16)
    qkv = jnp.dot(h, w_ref[...], preferred_element_type=F32) + b_ref[...]

    lanes = expand_ref.shape[1] // 3
    cs_hi, cs_lo, _ = _split3(cs_ref[...])
    tab = (jnp.dot(cs_hi, expand_ref[...], preferred_element_type=F32)
           + jnp.dot(cs_lo, expand_ref[...], preferred_element_type=F32))
    dim = lax.broadcasted_iota(jnp.int32, (1, lanes), 1) % HEAD_DIM
    cos = tab[:, :lanes] + (dim >= ROT_DIM).astype(F32)
    sin_a = tab[:, lanes:2 * lanes]
    sin_b = tab[:, 2 * lanes:]
    half = ROT_DIM // 2

    def norm_rope(z, gain, scale):
        ss = jnp.dot((z * z).astype(BF16), ones_ref[...], preferred_element_type=F32)
        z = z * lax.rsqrt(ss * (1.0 / HEAD_DIM) + NORM_EPS) * gain
        outs = []
        for c0 in range(0, z.shape[1], lanes):
            zz = z[:, c0:c0 + lanes]
            rot = (zz * cos + pltpu.roll(zz, lanes - half, 1) * sin_a + pltpu.roll(zz, half, 1) * sin_b)
            outs.append(rot * scale)
        return outs

    blk = 4 * HEAD_DIM
    for c0 in range(0, qd, blk):
        outs = norm_rope(qkv[:, c0:c0 + blk], qg_ref[...], HEAD_DIM ** -0.5 * LOG2_E)
        for j, o in enumerate(outs):
            hh = c0 // HEAD_DIM + 2 * j
            q_ref[0, hh] = o[:, :HEAD_DIM].astype(BF16)
            q_ref[0, hh + 1] = o[:, HEAD_DIM:].astype(BF16)
    outs = norm_rope(qkv[:, qd:qd + kd], kg_ref[...], 1.0)
    for j, o in enumerate(outs):
        k_ref[0, 2 * j] = o[:, :HEAD_DIM].astype(BF16)
        k_ref[0, 2 * j + 1] = o[:, HEAD_DIM:].astype(BF16)
    vv = qkv[:, qd + kd:]
    for j in range(N_KV_HEADS):
        v_ref[0, j] = vv[:, j * HEAD_DIM:(j + 1) * HEAD_DIM].astype(BF16)


def _qkv(x2, cs, g, w, b, qg, kg, expand, ones, bsz, seq, tt):
    n, d = x2.shape
    tps = seq // tt
    consts = [g, w, b, qg, kg, expand, ones]
    hm = lambda nh: pl.BlockSpec((1, nh, tt, HEAD_DIM), lambda i: (i // tps, 0, i % tps, 0))
    return pl.pallas_call(
        _qkv_kernel,
        grid=(n // tt,),
        in_specs=[pl.BlockSpec((tt, d), lambda i: (i, 0)), pl.BlockSpec((tt, ROT_DIM), lambda i: (i, 0))]
                 + [_const_spec(c.shape) for c in consts],
        out_specs=[hm(N_HEADS), hm(N_KV_HEADS), hm(N_KV_HEADS)],
        out_shape=[jax.ShapeDtypeStruct((bsz, N_HEADS, seq, HEAD_DIM), BF16),
                   jax.ShapeDtypeStruct((bsz, N_KV_HEADS, seq, HEAD_DIM), BF16),
                   jax.ShapeDtypeStruct((bsz, N_KV_HEADS, seq, HEAD_DIM), BF16)],
        compiler_params=_params(),
    )(x2, cs, *consts)


def _attn_kernel(sink_ref, q_ref, kc_ref, kp_ref, vc_ref, vp_ref, x_ref, w_ref, b_ref, out_ref, *, q_blocks):
    seq_start = pl.program_id(1) == 0
    blk = WINDOW
    rows = GROUP * blk
    t = lax.broadcasted_iota(jnp.int32, (rows, 2 * blk), 0) % blk
    c = lax.broadcasted_iota(jnp.int32, (rows, 2 * blk), 1)
    d = c - t
    band = (d >= 1) & (d <= blk)
    head_of_row = lax.broadcasted_iota(jnp.int32, (rows, 1), 0) // blk
    blocks = []
    for qb in range(q_blocks):
        if qb == 0:
            valid = band & ((c >= blk) | jnp.logical_not(seq_start))
        else:
            valid = band
        outs = []
        for g in range(N_KV_HEADS):
            q = q_ref[0, g * GROUP:(g + 1) * GROUP, qb * blk:(qb + 1) * blk, :].reshape(rows, HEAD_DIM)
            if qb == 0:
                kprev, vprev = kp_ref[0, g], vp_ref[0, g]
            else:
                kprev = kc_ref[0, g, (qb - 1) * blk:qb * blk, :]
                vprev = vc_ref[0, g, (qb - 1) * blk:qb * blk, :]
            k = jnp.concatenate([kprev, kc_ref[0, g, qb * blk:(qb + 1) * blk, :]], axis=0)
            v = jnp.concatenate([vprev, vc_ref[0, g, qb * blk:(qb + 1) * blk, :]], axis=0)
            s = lax.dot_general(q, k, NT_DIMS, preferred_element_type=F32)
            s = jnp.where(valid, s, -jnp.inf)
            sink = jnp.zeros((rows, 1), F32)
            for j in range(GROUP):
                sink = jnp.where(head_of_row == j, sink_ref[g * GROUP + j] * LOG2_E, sink)
            m = jnp.maximum(jnp.max(s, axis=-1, keepdims=True), sink)
            p = jnp.exp2(s - m)
            den = jnp.sum(p, axis=-1, keepdims=True) + jnp.exp2(sink - m)
            o = jnp.dot(p.astype(BF16), v, preferred_element_type=F32) / den
            outs.extend(o[j * blk:(j + 1) * blk] for j in range(GROUP))
        blocks.append(jnp.concatenate(outs, axis=1).astype(BF16))
    o_all = jnp.concatenate(blocks, axis=0)
    out_ref[...] = x_ref[...] + jnp.dot(o_all, w_ref[...], preferred_element_type=F32) + b_ref[...]


def _attention(sinks, q, k, v, x2, w_o, b_o, q_blocks):
    bsz, _, seq, _ = q.shape
    d = x2.shape[1]
    tq = q_blocks * WINDOW
    steps = seq // tq
    kern = functools.partial(_attn_kernel, q_blocks=q_blocks)
    cur = lambda nh: pl.BlockSpec((1, nh, tq, HEAD_DIM), lambda b, i: (b, 0, i, 0))
    prev = pl.BlockSpec((1, N_KV_HEADS, WINDOW, HEAD_DIM), lambda b, i: (b, 0, jnp.maximum(i * q_blocks - 1, 0), 0))
    rows = pl.BlockSpec((tq, d), lambda b, i: (b * steps + i, 0))
    return pl.pallas_call(
        kern,
        grid=(bsz, steps),
        in_specs=[pl.BlockSpec(memory_space=pltpu.SMEM), cur(N_HEADS), cur(N_KV_HEADS), prev, cur(N_KV_HEADS), prev,
                  rows, _const_spec(w_o.shape), _const_spec(b_o.shape)],
        out_specs=rows,
        out_shape=jax.ShapeDtypeStruct((bsz * seq, d), F32),
        compiler_params=_params(2),
    )(sinks, q, k, k, v, v, x2, w_o, b_o)


def _block_ones(width, group):
    idx = np.arange(width) // group
    return jnp.asarray(idx[:, None] == idx[None, :], BF16)


def _chunk_tril(rows):
    idx = np.arange(rows)
    same = (idx[:, None] // CHUNK) == (idx[None, :] // CHUNK)
    return jnp.asarray(same & (idx[None, :] <= idx[:, None]), BF16)


def _rope_expand(lanes=128):
    half = ROT_DIM // 2
    dim = np.arange(lanes) % HEAD_DIM
    f = np.arange(half)[:, None]
    hit = (dim[None, :] % half == f)
    zero = np.zeros((half, lanes))
    cos_rows = np.concatenate([hit & (dim < ROT_DIM)[None, :], zero, zero], axis=1)
    sin_rows = np.concatenate([zero, -1.0 * (hit & (dim < half)[None, :]),
                               hit & ((dim >= half) & (dim < ROT_DIM))[None, :]], axis=1)
    return jnp.asarray(np.concatenate([cos_rows, sin_rows], axis=0), BF16)


def _row(v):
    return v.reshape(1, -1)


def kernel(x, positions, ab_norm_g, ab_w_in, conv_in_b, conv_dw_w, conv_dw_b, conv_ln_g, conv_ln_b, rwkv_mu, rwkv_w0, rwkv_w2, rwkv_a0, rwkv_a2, rwkv_g2, rwkv_k_k, rwkv_k_a, rwkv_r_k, rwkv_ln_g, rwkv_ln_b, ab_w_out, attn_norm_g, attn_w_qkv, attn_b_qkv, attn_q_norm_g, attn_k_norm_g, attn_sinks, attn_w_o, attn_b_o, ffn_norm_g, ffn_w_up, ffn_conv_w, ffn_conv_b, ffn_w_down):
    bsz, seq, d = x.shape
    depth = ffn_norm_g.shape[0]
    n = bsz * seq
    tt = 512
    ffn_tt = 512
    x2 = x.reshape(n, d)
    ones_rwkv = _block_ones(RWKV_W, RWKV_HEAD_DIM)
    ones_quad = _block_ones(4 * HEAD_DIM, HEAD_DIM)
    tri = _chunk_tril(256)
    rope_expand = _rope_expand()
    ffn_g = ffn_norm_g[:, None, :]
    ffn_cb = ffn_conv_b[:, None, :]
    ffn_up = ffn_w_up.astype(BF16)
    ffn_down = ffn_w_down.astype(BF16)

    for layer in range(depth):
        i = layer // 2
        if layer % 2 == 0:
            zeros = jnp.zeros((LORA_DECAY, RWKV_W), F32)
            w2a2 = jnp.concatenate([jnp.concatenate([rwkv_w2[i], zeros], axis=1),
                                    jnp.concatenate([zeros, rwkv_a2[i]], axis=1)], axis=0).astype(BF16)
            c, ar, bk, vv, wc, gate, bonus = _front(
                x2, _row(ab_norm_g[i]), ab_w_in[i].astype(BF16),
                _row(conv_in_b[i]), conv_dw_w[i], _row(conv_dw_b[i]), _row(conv_ln_g[i]), _row(conv_ln_b[i]),
                _row(rwkv_mu[i]), _row(rwkv_w0[i]), w2a2, _row(rwkv_a0[i]), rwkv_g2[i].astype(BF16),
                _row(rwkv_k_k[i]), _row(rwkv_k_a[i]), _row(rwkv_r_k[i]), ones_rwkv, tri, bsz, seq, tt)
            y = _rwkv_core(ar, bk, vv, wc)
            x2 = _mix_ffn(x2, c, y, gate, bonus, _row(rwkv_ln_g[i]), _row(rwkv_ln_b[i]), ones_quad,
                          ab_w_out[i].astype(BF16), ffn_g, ffn_up, ffn_conv_w, ffn_cb, ffn_down, layer, seq, ffn_tt)
            continue
        else:
            half = ROT_DIM // 2
            inv_freq = ROPE_THETA ** (-(jnp.arange(half, dtype=F32) * 2.0) / ROT_DIM)
            lanes = 128
            cs = _rope_table(inv_freq, positions.astype(F32).reshape(n // lanes, lanes))
            cs = cs.reshape(ROT_DIM, n).T
            q, k, v = _qkv(x2, cs, _row(attn_norm_g[i]), attn_w_qkv[i].astype(BF16), _row(attn_b_qkv[i]),
                           _row(jnp.tile(attn_q_norm_g[i], 4)), _row(jnp.tile(attn_k_norm_g[i], 4)),
                           rope_expand, ones_quad, bsz, seq, tt)
            x2 = _attention(attn_sinks[i], q, k, v, x2, attn_w_o[i].astype(BF16), _row(attn_b_o[i]), 4)
        x2 = _ffn(x2, ffn_g, ffn_up, ffn_conv_w, ffn_cb, ffn_down, layer, seq, ffn_tt)
    return x2.reshape(bsz, seq, d)
```

```python
import functools

import jax
import jax.numpy as jnp
import numpy as np
from jax import lax
from jax.experimental import pallas as pl
from jax.experimental.pallas import tpu as pltpu

F32 = jnp.float32
BF16 = jnp.bfloat16

CONV_W = 512
CONV_K = 31
CONV_LN_EPS = 1e-5
RWKV_HEADS = 8
RWKV_HEAD_DIM = 64
RWKV_W = RWKV_HEADS * RWKV_HEAD_DIM
LORA_DECAY = 64
LORA_ICLR = 64
LORA_GATE = 128
RWKV_GN_EPS = RWKV_HEAD_DIM * 1e-5
RWKV_IN = 3 * RWKV_W + LORA_DECAY + LORA_ICLR + LORA_GATE
HEAD_DIM = 64
N_HEADS = 16
N_KV_HEADS = 4
GROUP = N_HEADS // N_KV_HEADS
WINDOW = 128
ROT_DIM = 16
ROPE_THETA = 500000.0
D_FF = 2816
FFN_CONV_K = 3
NORM_EPS = 1e-6
LOG2_E = 1.4426950408889634

CHUNK = 64
CORE_CHUNKS = 16
CORE_SPAN = 2
CORE_GROUP = 4
SUBLANES = 8
CONV_HALO = 32
ATTN_TOGETHER = 2
V7X_VMEM_LIMIT = 56 * 1024 * 1024

NT_DIMS = (((1,), (1,)), ((), ()))
TN_DIMS = (((0,), (0,)), ((), ()))


def _params(n_axes=1):
    return pltpu.CompilerParams(dimension_semantics=("arbitrary",) * n_axes,
                                vmem_limit_bytes=V7X_VMEM_LIMIT)


def _const_spec(shape):
    nd = len(shape)
    return pl.BlockSpec(shape, lambda *_: (0,) * nd, pipeline_mode=pl.Buffered(1))


def _rms_norm(x, g):
    return x * lax.rsqrt(jnp.mean(x * x, axis=-1, keepdims=True) + NORM_EPS) * g


def _sigmoid(z):
    return 1.0 / (1.0 + jnp.exp(-z))


def _split3(z):
    hi = z.astype(BF16)
    r1 = z - hi.astype(F32)
    mid = r1.astype(BF16)
    lo = (r1 - mid.astype(F32)).astype(BF16)
    return hi, mid, lo


def _conv_shift_copies(ext_ref, tt):
    span = tt + CONV_HALO - SUBLANES
    for s in range(1, SUBLANES):
        ext_ref[s, 0:span, :] = ext_ref[0, pl.ds(s, span), :]


def _conv_taps(ext_ref, dww_ref, dwb_ref, lng_ref, lnb_ref, c_ref, tt, row_block):
    first = CONV_HALO - (CONV_K - 1)
    for r0 in range(0, tt, row_block):
        acc = jnp.broadcast_to(dwb_ref[...], (row_block, CONV_W))
        for s in range(SUBLANES):
            taps = [j for j in range(CONV_K) if (first + j) % SUBLANES == s]
            lo = first + taps[0] - s
            hi = first + taps[-1] - s
            win = ext_ref[s, pl.ds(r0 + lo, row_block + hi - lo), :]
            for j in taps:
                a = first + j - s - lo
                acc = acc + dww_ref[j:j + 1, :] * win[a:a + row_block]
        mu = jnp.mean(acc, axis=-1, keepdims=True)
        xc = acc - mu
        var = jnp.mean(xc * xc, axis=-1, keepdims=True)
        z = xc * lax.rsqrt(var + CONV_LN_EPS) * lng_ref[...] + lnb_ref[...]
        c_ref[r0:r0 + row_block, :] = (z * _sigmoid(z)).astype(c_ref.dtype)


def _convmod(pc, inb, dww, dwb, lng, lnb, seq, tt):
    n = pc.shape[0]
    tps = seq // tt
    hb = tt // CONV_HALO
    kern = functools.partial(_convmod_kernel, tiles_per_seq=tps, row_block=64)
    return pl.pallas_call(
        kern,
        grid=(n // tt,),
        in_specs=[pl.BlockSpec((tt, 2 * CONV_W), lambda i: (i, 0)),
                  pl.BlockSpec((CONV_HALO, 2 * CONV_W), lambda i: (jnp.maximum(i * hb - 1, 0), 0)),
                  _const_spec(inb.shape), _const_spec(dww.shape), _const_spec(dwb.shape),
                  _const_spec(lng.shape), _const_spec(lnb.shape)],
        out_specs=pl.BlockSpec((tt, CONV_W), lambda i: (i, 0)),
        out_shape=jax.ShapeDtypeStruct((n, CONV_W), BF16),
        scratch_shapes=[pltpu.VMEM((SUBLANES, tt + CONV_HALO, CONV_W), F32)],
        compiler_params=_params(),
    )(pc, pc, inb, dww, dwb, lng, lnb)


def _rwkv_prep_body(ext_ref, mu_ref, w0_ref, w2a2_ref, a0_ref, g2_ref, kk_ref, ka_ref, rk_ref, ones_ref, tri_ref,
                    ar_ref, bk_ref, v_ref, wc_ref, gate_ref, bonus_ref, tt):
    rw = ext_ref[SUBLANES:, :]
    rw = rw + (ext_ref[pl.ds(SUBLANES - 1, tt), :] - rw) * mu_ref[...]

    r = rw[:, 0:RWKV_W]
    k = rw[:, RWKV_W:2 * RWKV_W]
    v = rw[:, 2 * RWKV_W:3 * RWKV_W]
    wa = rw[:, 3 * RWKV_W:3 * RWKV_W + LORA_DECAY + LORA_ICLR]
    gd = rw[:, 3 * RWKV_W + LORA_DECAY + LORA_ICLR:]

    lane = lax.broadcasted_iota(jnp.int32, wa.shape, 1)
    z = jnp.where(lane < LORA_DECAY, jnp.tanh(wa), wa)
    proj = jnp.dot(z.astype(BF16), w2a2_ref[...], preferred_element_type=F32)
    zw = -(w0_ref[...] + proj[:, :RWKV_W])
    softplus = jnp.maximum(zw, 0.0) + jnp.log(1.0 + jnp.exp(-jnp.abs(zw)))
    lw = -jnp.exp(-softplus - 0.5)
    a = _sigmoid(a0_ref[...] + proj[:, RWKV_W:])
    gate_ref[...] = jnp.dot(_sigmoid(gd).astype(BF16), g2_ref[...], preferred_element_type=F32)

    kk = k * kk_ref[...]
    ss = jnp.dot((kk * kk).astype(BF16), ones_ref[...], preferred_element_type=F32)
    kk = kk / jnp.maximum(jnp.sqrt(ss), 1e-12)
    k = k * (1.0 + (a - 1.0) * ka_ref[...])
    rk = jnp.dot((r * k * rk_ref[...]).astype(BF16), ones_ref[...], preferred_element_type=F32)
    bonus_ref[...] = rk * v

    half = tri_ref.shape[0]
    cums = []
    for r0 in range(0, tt, half):
        parts = _split3(lw[r0:r0 + half])
        cums.append(sum(jnp.dot(tri_ref[...], p, preferred_element_type=F32) for p in parts))
    cum = jnp.concatenate(cums, axis=0)
    e_cum = jnp.exp(cum)
    e_inv = jnp.exp(-cum)
    rt = r * e_cum
    at = -kk * jnp.exp(cum - lw)
    kt = k * e_inv
    bt = kk * a * e_inv

    nch = tt // CHUNK
    for c in range(nch):
        rows = slice(c * CHUNK, (c + 1) * CHUNK)
        ar_ref[0, c, 0:CHUNK, :] = at[rows].astype(BF16)
        ar_ref[0, c, CHUNK:, :] = rt[rows].astype(BF16)
        bk_ref[0, c, 0:CHUNK, :] = bt[rows].astype(BF16)
        bk_ref[0, c, CHUNK:, :] = kt[rows].astype(BF16)
        v_ref[0, c, :, :] = v[rows].astype(BF16)
        wc_ref[0, c, :, :] = jnp.broadcast_to(e_cum[(c + 1) * CHUNK - 1:(c + 1) * CHUNK], (SUBLANES, RWKV_W))


def _front_kernel(x0_ref, xn_ref, ng_ref, win_ref, inb_ref, dww_ref, dwb_ref, lng_ref, lnb_ref,
                  mu_ref, w0_ref, w2a2_ref, a0_ref, g2_ref, kk_ref, ka_ref, rk_ref, ones_ref, tri_ref,
                  c_ref, ar_ref, bk_ref, v_ref, wc_ref, gate_ref, bonus_ref, ext_ref, prc_ref,
                  *, tiles_per_seq, row_block):
    i = pl.program_id(0)
    tt = xn_ref.shape[0]

    def project(x_ref):
        h = _rms_norm(x_ref[...], ng_ref[...])
        return jnp.dot(h.astype(BF16), win_ref[...], preferred_element_type=F32)

    def make_current(p, conv_hist, shift_hist):
        z = p[:, :2 * CONV_W] + inb_ref[...]
        ext_ref[0, 0:CONV_HALO, :] = conv_hist
        ext_ref[0, CONV_HALO:, :] = z[:, :CONV_W] * _sigmoid(z[:, CONV_W:])
        _conv_shift_copies(ext_ref, tt)
        prc_ref[0:SUBLANES, :] = shift_hist
        prc_ref[SUBLANES:, :] = p[:, 2 * CONV_W:]

    @pl.when(i == 0)
    def _():
        make_current(project(x0_ref), jnp.zeros((CONV_HALO, CONV_W), F32), jnp.zeros((SUBLANES, RWKV_IN), F32))

    p_next = project(xn_ref)
    _conv_taps(ext_ref, dww_ref, dwb_ref, lng_ref, lnb_ref, c_ref, tt, row_block)
    _rwkv_prep_body(prc_ref, mu_ref, w0_ref, w2a2_ref, a0_ref, g2_ref, kk_ref, ka_ref, rk_ref, ones_ref, tri_ref,
                    ar_ref, bk_ref, v_ref, wc_ref, gate_ref, bonus_ref, tt)
    same_seq = (i + 1) % tiles_per_seq != 0
    make_current(p_next,
                 jnp.where(same_seq, ext_ref[0, tt:tt + CONV_HALO, :], 0.0),
                 jnp.where(same_seq, prc_ref[tt:tt + SUBLANES, :], 0.0))


def _front(x2, ng, w_in, inb, dww, dwb, lng, lnb, mu, w0, w2a2, a0, g2, k_k, k_a, r_k, ones, tri, bsz, seq, tt):
    n, d = x2.shape
    tps = seq // tt
    steps = n // tt
    nch = tt // CHUNK
    nc = seq // CHUNK
    kern = functools.partial(_front_kernel, tiles_per_seq=tps, row_block=64)
    hm = lambda rows: pl.BlockSpec((1, nch, rows, RWKV_W), lambda i: (i // tps, i % tps, 0, 0))
    hm_shape = lambda rows, dt: jax.ShapeDtypeStruct((bsz, nc, rows, RWKV_W), dt)
    consts = [ng, w_in, inb, dww, dwb, lng, lnb, mu, w0, w2a2, a0, g2, k_k, k_a, r_k, ones, tri]
    row = lambda w: pl.BlockSpec((tt, w), lambda i: (i, 0))
    return pl.pallas_call(
        kern,
        grid=(steps,),
        in_specs=[pl.BlockSpec((tt, d), lambda i: (0, 0)),
                  pl.BlockSpec((tt, d), lambda i: (jnp.minimum(i + 1, steps - 1), 0))]
                 + [_const_spec(c.shape) for c in consts],
        out_specs=[row(CONV_W), hm(2 * CHUNK), hm(2 * CHUNK), hm(CHUNK), hm(SUBLANES), row(RWKV_W), row(RWKV_W)],
        out_shape=[jax.ShapeDtypeStruct((n, CONV_W), BF16),
                   hm_shape(2 * CHUNK, BF16), hm_shape(2 * CHUNK, BF16), hm_shape(CHUNK, BF16), hm_shape(SUBLANES, F32),
                   jax.ShapeDtypeStruct((n, RWKV_W), F32), jax.ShapeDtypeStruct((n, RWKV_W), F32)],
        scratch_shapes=[pltpu.VMEM((SUBLANES, tt + CONV_HALO, CONV_W), F32),
                        pltpu.VMEM((tt + SUBLANES, RWKV_IN), F32)],
        compiler_params=_params(),
    )(x2, x2, *consts)


def _rwkv_core_kernel(ar_ref, bk_ref, v_ref, wc_ref, y_ref, ht_ref):
    @pl.when(pl.program_id(0) == 0)
    def _():
        ht_ref[...] = jnp.zeros_like(ht_ref)

    bsz, nchunks = ar_ref.shape[0], ar_ref.shape[1]
    gw = CORE_GROUP * RWKV_HEAD_DIM
    ngroups = RWKV_W // gw
    npar = bsz * ngroups
    row = lax.broadcasted_iota(jnp.int32, (CHUNK, gw), 0)
    col = lax.broadcasted_iota(jnp.int32, (CHUNK, gw), 1) % RWKV_HEAD_DIM
    strict = col < row
    incl = col <= row
    eye = (col == row).astype(F32)
    blk_r = lax.broadcasted_iota(jnp.int32, (gw, gw), 0) // RWKV_HEAD_DIM
    blk_c = lax.broadcasted_iota(jnp.int32, (gw, gw), 1) // RWKV_HEAD_DIM
    same_head = blk_r == blk_c
    head_of_lane = lax.broadcasted_iota(jnp.int32, (CHUNK, gw), 1) // RWKV_HEAD_DIM
    bmm = functools.partial(jnp.einsum, preferred_element_type=F32)

    def block_diag(z):
        tiled = jnp.concatenate([z] * CORE_GROUP, axis=1)
        return jnp.where(same_head[None], tiled, jnp.zeros_like(tiled))

    def grouped(ref, g0):
        return jnp.stack([ref[b, g, :, gi * gw:(gi + 1) * gw]
                          for g in range(g0, g0 + span) for b in range(bsz) for gi in range(ngroups)], axis=0)

    def state_free_part(g0, outs):
        ar = grouped(ar_ref, g0)
        bk = grouped(bk_ref, g0)
        vv = grouped(v_ref, g0)
        rhs = jnp.concatenate([block_diag(bk[:, 0:CHUNK]), block_diag(bk[:, CHUNK:])], axis=1)
        s = bmm('bmk,bnk->bmn', ar, rhs)
        yield
        a_ab = jnp.where(strict[None], s[:, 0:CHUNK, 0:gw], 0.0)
        a_ak = jnp.where(strict[None], s[:, 0:CHUNK, gw:], 0.0).astype(BF16)
        a_rb = jnp.where(incl[None], s[:, CHUNK:, 0:gw], 0.0).astype(BF16)
        a_rk = jnp.where(incl[None], s[:, CHUNK:, gw:], 0.0).astype(BF16)
        p = eye[None] + a_ab
        qb = a_ab.astype(BF16)
        q = bmm('bij,bjk->bik', qb, block_diag(qb))
        yield
        for _ in range(4):
            qb = q.astype(BF16)
            both = bmm('bij,bjk->bik', jnp.concatenate([qb, p.astype(BF16)], axis=1), block_diag(qb))
            yield
            q = both[:, 0:CHUNK]
            p = p + both[:, CHUNK:]
        p = p + bmm('bij,bjk->bik', p.astype(BF16), block_diag(q.astype(BF16)))
        yield
        bdv = block_diag(vv)
        av = bmm('bij,bjk->bik', a_ak, bdv)
        tb = p.astype(BF16)
        for j in range(span):
            sl = slice(j * npar, (j + 1) * npar)
            outs[g0 + j].update(ar=ar[sl], bk=bk[sl], vv=vv[sl], a_rb=a_rb[sl], a_rk=a_rk[sl], tb=tb[sl],
                                bdv=bdv[sl], av=av[sl])
        yield

    def recurrent_part(g0, outs):
        for g in range(g0, g0 + span):
            yield from recurrent_chunk(g, outs[g])

    def recurrent_chunk(g, c):
        ht = ht_ref[...]
        arh = bmm('bmk,bnk->bmn', c['ar'], block_diag(ht.astype(BF16)))
        yield
        u = bmm('bij,bjk->bik', c['tb'], block_diag((arh[:, 0:CHUNK] + c['av']).astype(BF16)))
        yield
        ub = u.astype(BF16)
        y = (arh[:, CHUNK:] + bmm('bij,bjk->bik', c['a_rb'], block_diag(ub))
             + bmm('bij,bjk->bik', c['a_rk'], c['bdv']))
        cross = bmm('bsi,bsj->bij', jnp.concatenate([ub, c['vv']], axis=1), c['bk'])
        yield
        upd = sum(jnp.where(head_of_lane == h, cross[:, h * RWKV_HEAD_DIM:(h + 1) * RWKV_HEAD_DIM], 0.0)
                  for h in range(CORE_GROUP))
        for b in range(bsz):
            for gi in range(ngroups):
                n = b * ngroups + gi
                lanes = slice(gi * gw, (gi + 1) * gw)
                ht_ref[n] = wc_ref[b, g, 0:1, lanes] * (ht[n] + upd[n])
                y_ref[b, g * CHUNK:(g + 1) * CHUNK, lanes] = y[n]

    def run_together(*gens):
        live = list(gens)
        while live:
            for gen in list(live):
                if next(gen, StopIteration) is StopIteration:
                    live.remove(gen)

    span = CORE_SPAN
    chunks = [dict() for _ in range(nchunks)]
    run_together(state_free_part(0, chunks))
    for g0 in range(span, nchunks, span):
        run_together(state_free_part(g0, chunks), recurrent_part(g0 - span, chunks))
    run_together(recurrent_part(nchunks - span, chunks))


def _rwkv_core(ar, bk, v, wc):
    bsz, nc = ar.shape[:2]
    g = CORE_CHUNKS
    gw = CORE_GROUP * RWKV_HEAD_DIM
    blk = lambda rows: pl.BlockSpec((bsz, g, rows, RWKV_W), lambda i: (0, i, 0, 0))
    y = pl.pallas_call(
        _rwkv_core_kernel,
        grid=(nc // g,),
        in_specs=[blk(2 * CHUNK), blk(2 * CHUNK), blk(CHUNK), blk(SUBLANES)],
        out_specs=pl.BlockSpec((bsz, g * CHUNK, RWKV_W), lambda i: (0, i, 0)),
        out_shape=jax.ShapeDtypeStruct((bsz, nc * CHUNK, RWKV_W), F32),
        scratch_shapes=[pltpu.VMEM((bsz * (RWKV_W // gw), RWKV_HEAD_DIM, gw), F32)],
        compiler_params=_params(),
    )(ar, bk, v, wc)
    return y.reshape(bsz * nc * CHUNK, RWKV_W)


def _mix_ffn_kernel(x_ref, c_ref, y_ref, gate_ref, bonus_ref, lng_ref, lnb_ref, ones_ref, w_ref,
                    g_ref, wup_ref, cw_ref, cb_ref, wdn_ref, o_ref, hist_ref, gs_ref, *, tiles_per_seq):
    i = pl.program_id(0)
    tt = x_ref.shape[0]

    @pl.when(i == 0)
    def _():
        hist_ref[...] = jnp.zeros_like(hist_ref)

    inv_n = 1.0 / RWKV_HEAD_DIM
    y = y_ref[...]
    half = ones_ref.shape[0]

    def head_sum(z):
        zb = z.astype(BF16)
        return jnp.concatenate([jnp.dot(zb[:, c0:c0 + half], ones_ref[...], preferred_element_type=F32)
                                for c0 in range(0, z.shape[1], half)], axis=1)

    yc = y - head_sum(y) * inv_n
    var = head_sum(yc * yc) * inv_n
    yn = yc * lax.rsqrt(var + RWKV_GN_EPS) * lng_ref[...] + lnb_ref[...]
    yo = ((yn + bonus_ref[...]) * gate_ref[...]).astype(BF16)
    mix = (jnp.dot(c_ref[...], w_ref[0:CONV_W, :], preferred_element_type=F32)
           + jnp.dot(yo, w_ref[CONV_W:, :], preferred_element_type=F32))
    x1 = x_ref[...] + mix
    hist = jnp.where(i % tiles_per_seq != 0, hist_ref[...], 0.0)
    o_ref[...] = _ffn_tile(x1, hist, g_ref, wup_ref, cw_ref, cb_ref, wdn_ref, gs_ref)
    hist_ref[...] = x1[tt - SUBLANES:, :]


def _mix_ffn(x2, c, y, gate, bonus, lng, lnb, ones, w_out, g, wup, cw, cb, wdn, layer, seq, tt):
    n, d = x2.shape
    row = lambda w: pl.BlockSpec((tt, w), lambda i: (i, 0))
    consts = [lng, lnb, ones, w_out]
    ffn_consts = [g, wup, cw, cb, wdn]
    kern = functools.partial(_mix_ffn_kernel, tiles_per_seq=seq // tt)
    return pl.pallas_call(
        kern,
        grid=(n // tt,),
        in_specs=[row(d), row(CONV_W), row(RWKV_W), row(RWKV_W), row(RWKV_W)]
                 + [_const_spec(c_.shape) for c_ in consts] + [_layer_spec(c_, layer) for c_ in ffn_consts],
        out_specs=row(d),
        out_shape=jax.ShapeDtypeStruct((n, d), F32),
        scratch_shapes=[pltpu.VMEM((SUBLANES, d), F32), pltpu.VMEM((tt + SUBLANES, D_FF), F32)],
        compiler_params=_params(),
    )(x2, c, y, gate, bonus, *consts, *ffn_consts)


def _ffn_tile(x, hist, g_ref, wup_ref, cw_ref, cb_ref, wdn_ref, gs_ref):
    tt = x.shape[0]
    xe = jnp.concatenate([hist, x], axis=0)
    h = _rms_norm(xe, g_ref[...]).astype(BF16)
    u = jnp.dot(h, wup_ref[...], preferred_element_type=F32)
    gs_ref[...] = u[:, :D_FF]
    gate = cb_ref[...]
    for j in range(FFN_CONV_K):
        gate = gate + cw_ref[j:j + 1, :] * gs_ref[pl.ds(SUBLANES - (FFN_CONV_K - 1) + j, tt), :]
    act = (gate * _sigmoid(gate) * u[SUBLANES:, D_FF:]).astype(BF16)
    return x + jnp.dot(act, wdn_ref[...], preferred_element_type=F32)


def _ffn_kernel(x_ref, halo_ref, g_ref, wup_ref, cw_ref, cb_ref, wdn_ref, o_ref, gs_ref, *, tiles_per_seq):
    keep = (pl.program_id(0) % tiles_per_seq != 0).astype(F32)
    o_ref[...] = _ffn_tile(x_ref[...], halo_ref[...] * keep, g_ref, wup_ref, cw_ref, cb_ref, wdn_ref, gs_ref)


def _layer_spec(stacked, layer):
    tail = stacked.shape[1:]
    return pl.BlockSpec((None,) + tail, lambda *_: (layer,) + (0,) * len(tail), pipeline_mode=pl.Buffered(1))


def _ffn(x2, g, wup, cw, cb, wdn, layer, seq, tt):
    n, d = x2.shape
    tps = seq // tt
    hb = tt // SUBLANES
    kern = functools.partial(_ffn_kernel, tiles_per_seq=tps)
    consts = [g, wup, cw, cb, wdn]
    return pl.pallas_call(
        kern,
        grid=(n // tt,),
        in_specs=[pl.BlockSpec((tt, d), lambda i: (i, 0)),
                  pl.BlockSpec((SUBLANES, d), lambda i: (jnp.maximum(i * hb - 1, 0), 0))]
                 + [_layer_spec(c, layer) for c in consts],
        out_specs=pl.BlockSpec((tt, d), lambda i: (i, 0)),
        out_shape=jax.ShapeDtypeStruct((n, d), F32),
        scratch_shapes=[pltpu.VMEM((tt + SUBLANES, D_FF), F32)],
        compiler_params=_params(),
    )(x2, x2, *consts)


def _rope_table_kernel(freq_ref, pos_ref, cs_ref):
    half = ROT_DIM // 2
    pos = pos_ref[...]
    for f in range(half):
        ang = pos * freq_ref[f]
        cs_ref[f] = jnp.cos(ang)
        cs_ref[half + f] = jnp.sin(ang)


def _rope_table(inv_freq, pos_dense):
    rows, lanes = pos_dense.shape
    return pl.pallas_call(
        _rope_table_kernel,
        in_specs=[pl.BlockSpec(memory_space=pltpu.SMEM), pl.BlockSpec((rows, lanes), lambda: (0, 0))],
        out_specs=pl.BlockSpec((ROT_DIM, rows, lanes), lambda: (0, 0, 0)),
        out_shape=jax.ShapeDtypeStruct((ROT_DIM, rows, lanes), F32),
    )(inv_freq, pos_dense)


def _qkv_kernel(x_ref, cs_ref, g_ref, w_ref, b_ref, qg_ref, kg_ref, expand_ref, ones_ref,
                q_ref, k_ref, v_ref):
    qd = N_HEADS * HEAD_DIM
    kd = N_KV_HEADS * HEAD_DIM
    h = _rms_norm(x_ref[...], g_ref[...]).astype(BF16)

    def project(c0, width):
        return jnp.dot(h, w_ref[:, c0:c0 + width], preferred_element_type=F32) + b_ref[:, c0:c0 + width]

    lanes = expand_ref.shape[1] // 3
    cs_hi, cs_lo, _ = _split3(cs_ref[...])
    tab = (jnp.dot(cs_hi, expand_ref[...], preferred_element_type=F32)
           + jnp.dot(cs_lo, expand_ref[...], preferred_element_type=F32))
    dim = lax.broadcasted_iota(jnp.int32, (1, lanes), 1) % HEAD_DIM
    cos = tab[:, :lanes] + (dim >= ROT_DIM).astype(F32)
    sin_a = tab[:, lanes:2 * lanes]
    sin_b = tab[:, 2 * lanes:]
    half = ROT_DIM // 2

    def norm_rope(c0, gain, scale, dst_ref, head0):
        z = project(c0, blk)
        yield
        ss = jnp.dot((z * z).astype(BF16), ones_ref[...], preferred_element_type=F32)
        yield
        z = z * lax.rsqrt(ss * (1.0 / HEAD_DIM) + NORM_EPS) * gain
        for j, c0 in enumerate(range(0, z.shape[1], lanes)):
            zz = z[:, c0:c0 + lanes]
            rot = (zz * cos + pltpu.roll(zz, lanes - half, 1) * sin_a + pltpu.roll(zz, half, 1) * sin_b) * scale
            dst_ref[0, head0 + 2 * j] = rot[:, :HEAD_DIM].astype(BF16)
            dst_ref[0, head0 + 2 * j + 1] = rot[:, HEAD_DIM:].astype(BF16)
            yield

    def values():
        vt = project(qd + kd, kd).T
        yield
        for j in range(N_KV_HEADS):
            v_ref[0, j] = vt[j * HEAD_DIM:(j + 1) * HEAD_DIM, :].astype(BF16)
        yield

    blk = 4 * HEAD_DIM
    live = [norm_rope(c0, qg_ref[...], HEAD_DIM ** -0.5 * LOG2_E, q_ref, c0 // HEAD_DIM) for c0 in range(0, qd, blk)]
    live += [norm_rope(qd, kg_ref[...], 1.0, k_ref, 0), values()]
    while live:
        live = [u for u in live if next(u, StopIteration) is not StopIteration]


def _qkv(x2, cs, g, w, b, qg, kg, expand, ones, bsz, seq, tt):
    n, d = x2.shape
    tps = seq // tt
    consts = [g, w, b, qg, kg, expand, ones]
    hm = lambda nh: pl.BlockSpec((1, nh, tt, HEAD_DIM), lambda i: (i // tps, 0, i % tps, 0))
    return pl.pallas_call(
        _qkv_kernel,
        grid=(n // tt,),
        in_specs=[pl.BlockSpec((tt, d), lambda i: (i, 0)), pl.BlockSpec((tt, ROT_DIM), lambda i: (i, 0))]
                 + [_const_spec(c.shape) for c in consts],
        out_specs=[hm(N_HEADS), hm(N_KV_HEADS),
                   pl.BlockSpec((1, N_KV_HEADS, HEAD_DIM, tt), lambda i: (i // tps, 0, 0, i % tps))],
        out_shape=[jax.ShapeDtypeStruct((bsz, N_HEADS, seq, HEAD_DIM), BF16),
                   jax.ShapeDtypeStruct((bsz, N_KV_HEADS, seq, HEAD_DIM), BF16),
                   jax.ShapeDtypeStruct((bsz, N_KV_HEADS, HEAD_DIM, seq), BF16)],
        compiler_params=_params(),
    )(x2, cs, *consts)


def _attn_kernel(sink_ref, q_ref, kc_ref, kp_ref, vc_ref, vp_ref, x_ref, w_ref, b_ref, out_ref, *, q_blocks):
    seq_start = pl.program_id(1) == 0
    blk = WINDOW
    cols = GROUP * blk
    c = lax.broadcasted_iota(jnp.int32, (2 * blk, cols), 0)
    t = lax.broadcasted_iota(jnp.int32, (2 * blk, cols), 1) % blk
    d = c - t
    band = (d >= 1) & (d <= blk)
    head_of_lane = lax.broadcasted_iota(jnp.int32, (1, cols), 1) // blk
    def unit(qb, g, outs):
        valid = band & ((c >= blk) | jnp.logical_not(seq_start)) if qb == 0 else band
        q = q_ref[0, g * GROUP:(g + 1) * GROUP, qb * blk:(qb + 1) * blk, :].reshape(cols, HEAD_DIM)
        if qb == 0:
            kprev, vprev = kp_ref[0, g], vp_ref[0, g]
        else:
            kprev = kc_ref[0, g, (qb - 1) * blk:qb * blk, :]
            vprev = vc_ref[0, g, :, (qb - 1) * blk:qb * blk]
        k = jnp.concatenate([kprev, kc_ref[0, g, qb * blk:(qb + 1) * blk, :]], axis=0)
        vt = jnp.concatenate([vprev, vc_ref[0, g, :, qb * blk:(qb + 1) * blk]], axis=1)
        s = lax.dot_general(k, q, NT_DIMS, preferred_element_type=F32)
        yield
        s = jnp.where(valid, s, -jnp.inf)
        sink = jnp.zeros((1, cols), F32)
        for j in range(GROUP):
            sink = jnp.where(head_of_lane == j, sink_ref[g * GROUP + j] * LOG2_E, sink)
        m = jnp.maximum(jnp.max(s, axis=0, keepdims=True), sink)
        p = jnp.exp2(s - m)
        den = jnp.sum(p, axis=0, keepdims=True) + jnp.exp2(sink - m)
        yield
        ot = jnp.dot(vt, p.astype(BF16), preferred_element_type=F32) / den
        yield
        o = ot.T
        outs[g] = [o[j * blk:(j + 1) * blk] for j in range(GROUP)]
        yield

    blocks = []
    for qb0 in range(0, q_blocks, ATTN_TOGETHER):
        qbs = range(qb0, min(qb0 + ATTN_TOGETHER, q_blocks))
        outs = {qb: [None] * N_KV_HEADS for qb in qbs}
        live = [unit(qb, g, outs[qb]) for qb in qbs for g in range(N_KV_HEADS)]
        while live:
            live = [u for u in live if next(u, StopIteration) is not StopIteration]
        for qb in qbs:
            blocks.append(jnp.concatenate([o for group in outs[qb] for o in group], axis=1).astype(BF16))
    o_all = jnp.concatenate(blocks, axis=0)
    out_ref[...] = x_ref[...] + jnp.dot(o_all, w_ref[...], preferred_element_type=F32) + b_ref[...]


def _attention(sinks, q, k, v, x2, w_o, b_o, q_blocks):
    bsz, _, seq, _ = q.shape
    d = x2.shape[1]
    tq = q_blocks * WINDOW
    steps = seq // tq
    kern = functools.partial(_attn_kernel, q_blocks=q_blocks)
    cur = lambda nh: pl.BlockSpec((1, nh, tq, HEAD_DIM), lambda b, i: (b, 0, i, 0))
    prev = pl.BlockSpec((1, N_KV_HEADS, WINDOW, HEAD_DIM), lambda b, i: (b, 0, jnp.maximum(i * q_blocks - 1, 0), 0))
    rows = pl.BlockSpec((tq, d), lambda b, i: (b * steps + i, 0))
    vcur = pl.BlockSpec((1, N_KV_HEADS, HEAD_DIM, tq), lambda b, i: (b, 0, 0, i))
    vprev = pl.BlockSpec((1, N_KV_HEADS, HEAD_DIM, WINDOW), lambda b, i: (b, 0, 0, jnp.maximum(i * q_blocks - 1, 0)))
    return pl.pallas_call(
        kern,
        grid=(bsz, steps),
        in_specs=[pl.BlockSpec(memory_space=pltpu.SMEM), cur(N_HEADS), cur(N_KV_HEADS), prev, vcur, vprev,
                  rows, _const_spec(w_o.shape), _const_spec(b_o.shape)],
        out_specs=rows,
        out_shape=jax.ShapeDtypeStruct((bsz * seq, d), F32),
        compiler_params=_params(2),
    )(sinks, q, k, k, v, v, x2, w_o, b_o)


def _block_ones(width, group):
    idx = np.arange(width) // group
    return jnp.asarray(idx[:, None] == idx[None, :], BF16)


def _chunk_tril(rows):
    idx = np.arange(rows)
    same = (idx[:, None] // CHUNK) == (idx[None, :] // CHUNK)
    return jnp.asarray(same & (idx[None, :] <= idx[:, None]), BF16)


def _rope_expand(lanes=128):
    half = ROT_DIM // 2
    dim = np.arange(lanes) % HEAD_DIM
    f = np.arange(half)[:, None]
    hit = (dim[None, :] % half == f)
    zero = np.zeros((half, lanes))
    cos_rows = np.concatenate([hit & (dim < ROT_DIM)[None, :], zero, zero], axis=1)
    sin_rows = np.concatenate([zero, -1.0 * (hit & (dim < half)[None, :]),
                               hit & ((dim >= half) & (dim < ROT_DIM))[None, :]], axis=1)
    return jnp.asarray(np.concatenate([cos_rows, sin_rows], axis=0), BF16)


def _row(v):
    return v.reshape(1, -1)


def kernel(x, positions, ab_norm_g, ab_w_in, conv_in_b, conv_dw_w, conv_dw_b, conv_ln_g, conv_ln_b, rwkv_mu, rwkv_w0, rwkv_w2, rwkv_a0, rwkv_a2, rwkv_g2, rwkv_k_k, rwkv_k_a, rwkv_r_k, rwkv_ln_g, rwkv_ln_b, ab_w_out, attn_norm_g, attn_w_qkv, attn_b_qkv, attn_q_norm_g, attn_k_norm_g, attn_sinks, attn_w_o, attn_b_o, ffn_norm_g, ffn_w_up, ffn_conv_w, ffn_conv_b, ffn_w_down):
    bsz, seq, d = x.shape
    depth = ffn_norm_g.shape[0]
    n = bsz * seq
    tt = 512
    ffn_tt = 512
    x2 = x.reshape(n, d)
    ones_rwkv = _block_ones(RWKV_W, RWKV_HEAD_DIM)
    ones_quad = _block_ones(4 * HEAD_DIM, HEAD_DIM)
    tri = _chunk_tril(256)
    rope_expand = _rope_expand()
    ffn_g = ffn_norm_g[:, None, :]
    ffn_cb = ffn_conv_b[:, None, :]
    ffn_up = ffn_w_up.astype(BF16)
    ffn_down = ffn_w_down.astype(BF16)

    for layer in range(depth):
        i = layer // 2
        if layer % 2 == 0:
            zeros = jnp.zeros((LORA_DECAY, RWKV_W), F32)
            w2a2 = jnp.concatenate([jnp.concatenate([rwkv_w2[i], zeros], axis=1),
                                    jnp.concatenate([zeros, rwkv_a2[i]], axis=1)], axis=0).astype(BF16)
            c, ar, bk, vv, wc, gate, bonus = _front(
                x2, _row(ab_norm_g[i]), ab_w_in[i].astype(BF16),
                _row(conv_in_b[i]), conv_dw_w[i], _row(conv_dw_b[i]), _row(conv_ln_g[i]), _row(conv_ln_b[i]),
                _row(rwkv_mu[i]), _row(rwkv_w0[i]), w2a2, _row(rwkv_a0[i]), rwkv_g2[i].astype(BF16),
                _row(rwkv_k_k[i]), _row(rwkv_k_a[i]), _row(rwkv_r_k[i]), ones_rwkv, tri, bsz, seq, tt)
            y = _rwkv_core(ar, bk, vv, wc)
            x2 = _mix_ffn(x2, c, y, gate, bonus, _row(rwkv_ln_g[i]), _row(rwkv_ln_b[i]), ones_quad,
                          ab_w_out[i].astype(BF16), ffn_g, ffn_up, ffn_conv_w, ffn_cb, ffn_down, layer, seq, ffn_tt)
            continue
        else:
            half = ROT_DIM // 2
            inv_freq = ROPE_THETA ** (-(jnp.arange(half, dtype=F32) * 2.0) / ROT_DIM)
            lanes = 128
            cs = _rope_table(inv_freq, positions.astype(F32).reshape(n // lanes, lanes))
            cs = cs.reshape(ROT_DIM, n).T
            q, k, v = _qkv(x2, cs, _row(attn_norm_g[i]), attn_w_qkv[i].astype(BF16), _row(attn_b_qkv[i]),
                           _row(jnp.tile(attn_q_norm_g[i], 4)), _row(jnp.tile(attn_k_norm_g[i], 4)),
                           rope_expand, ones_quad, bsz, seq, tt)
            x2 = _attention(attn_sinks[i], q, k, v, x2, attn_w_o[i].astype(BF16), _row(attn_b_o[i]), 4)
        x2 = _ffn(x2, ffn_g, ffn_up, ffn_conv_w, ffn_cb, ffn_down, layer, seq, ffn_tt)
    return x2.reshape(bsz, seq, d)
```

```python
import functools

import jax
import jax.numpy as jnp
import numpy as np
from jax import lax
from jax.experimental import pallas as pl
from jax.experimental.pallas import tpu as pltpu

F32 = jnp.float32
BF16 = jnp.bfloat16

CONV_W = 512
CONV_K = 31
CONV_LN_EPS = 1e-5
RWKV_HEADS = 8
RWKV_HEAD_DIM = 64
RWKV_W = RWKV_HEADS * RWKV_HEAD_DIM
LORA_DECAY = 64
LORA_ICLR = 64
LORA_GATE = 128
RWKV_GN_EPS = RWKV_HEAD_DIM * 1e-5
RWKV_IN = 3 * RWKV_W + LORA_DECAY + LORA_ICLR + LORA_GATE
HEAD_DIM = 64
N_HEADS = 16
N_KV_HEADS = 4
GROUP = N_HEADS // N_KV_HEADS
WINDOW = 128
ROT_DIM = 16
ROPE_THETA = 500000.0
D_FF = 2816
FFN_CONV_K = 3
NORM_EPS = 1e-6
LOG2_E = 1.4426950408889634

CHUNK = 64
CORE_CHUNKS = 16
CORE_SPAN = 2
CORE_GROUP = 4
SUBLANES = 8
CONV_HALO = 32
ATTN_TOGETHER = 4
V7X_VMEM_LIMIT = 56 * 1024 * 1024

NT_DIMS = (((1,), (1,)), ((), ()))


def _params(n_axes=1):
    return pltpu.CompilerParams(dimension_semantics=("arbitrary",) * n_axes,
                                vmem_limit_bytes=V7X_VMEM_LIMIT)


def _const_spec(shape):
    nd = len(shape)
    return pl.BlockSpec(shape, lambda *_: (0,) * nd, pipeline_mode=pl.Buffered(1))


def _rms_norm(x, g):
    return x * lax.rsqrt(jnp.mean(x * x, axis=-1, keepdims=True) + NORM_EPS) * g


def _sigmoid(z):
    return 1.0 / (1.0 + jnp.exp(-z))


def _split3(z):
    hi = z.astype(BF16)
    r1 = z - hi.astype(F32)
    mid = r1.astype(BF16)
    lo = (r1 - mid.astype(F32)).astype(BF16)
    return hi, mid, lo


def _conv_shift_copies(ext_ref, tt):
    span = tt + CONV_HALO - SUBLANES
    for s in range(1, SUBLANES):
        ext_ref[s, 0:span, :] = ext_ref[0, pl.ds(s, span), :]


def _conv_taps(ext_ref, dww_ref, dwb_ref, lng_ref, lnb_ref, c_ref, tt, row_block):
    first = CONV_HALO - (CONV_K - 1)
    for r0 in range(0, tt, row_block):
        acc = jnp.broadcast_to(dwb_ref[...], (row_block, CONV_W))
        for s in range(SUBLANES):
            taps = [j for j in range(CONV_K) if (first + j) % SUBLANES == s]
            lo = first + taps[0] - s
            hi = first + taps[-1] - s
            win = ext_ref[s, pl.ds(r0 + lo, row_block + hi - lo), :]
            for j in taps:
                a = first + j - s - lo
                acc = acc + dww_ref[j:j + 1, :] * win[a:a + row_block]
        mu = jnp.mean(acc, axis=-1, keepdims=True)
        xc = acc - mu
        var = jnp.mean(xc * xc, axis=-1, keepdims=True)
        z = xc * lax.rsqrt(var + CONV_LN_EPS) * lng_ref[...] + lnb_ref[...]
        c_ref[r0:r0 + row_block, :] = (z * _sigmoid(z)).astype(c_ref.dtype)


def _convmod(pc, inb, dww, dwb, lng, lnb, seq, tt):
    n = pc.shape[0]
    tps = seq // tt
    hb = tt // CONV_HALO
    kern = functools.partial(_convmod_kernel, tiles_per_seq=tps, row_block=64)
    return pl.pallas_call(
        kern,
        grid=(n // tt,),
        in_specs=[pl.BlockSpec((tt, 2 * CONV_W), lambda i: (i, 0)),
                  pl.BlockSpec((CONV_HALO, 2 * CONV_W), lambda i: (jnp.maximum(i * hb - 1, 0), 0)),
                  _const_spec(inb.shape), _const_spec(dww.shape), _const_spec(dwb.shape),
                  _const_spec(lng.shape), _const_spec(lnb.shape)],
        out_specs=pl.BlockSpec((tt, CONV_W), lambda i: (i, 0)),
        out_shape=jax.ShapeDtypeStruct((n, CONV_W), BF16),
        scratch_shapes=[pltpu.VMEM((SUBLANES, tt + CONV_HALO, CONV_W), F32)],
        compiler_params=_params(),
    )(pc, pc, inb, dww, dwb, lng, lnb)


def _rwkv_prep_body(ext_ref, mu_ref, w0_ref, w2a2_ref, a0_ref, g2_ref, kk_ref, ka_ref, rk_ref, ones_ref, tri_ref,
                    ar_ref, bk_ref, v_ref, wc_ref, gate_ref, bonus_ref, tt):
    rw = ext_ref[SUBLANES:, :]
    rw = rw + (ext_ref[pl.ds(SUBLANES - 1, tt), :] - rw) * mu_ref[...]

    r = rw[:, 0:RWKV_W]
    k = rw[:, RWKV_W:2 * RWKV_W]
    v = rw[:, 2 * RWKV_W:3 * RWKV_W]
    wa = rw[:, 3 * RWKV_W:3 * RWKV_W + LORA_DECAY + LORA_ICLR]
    gd = rw[:, 3 * RWKV_W + LORA_DECAY + LORA_ICLR:]

    lane = lax.broadcasted_iota(jnp.int32, wa.shape, 1)
    z = jnp.where(lane < LORA_DECAY, jnp.tanh(wa), wa)
    proj = jnp.dot(z.astype(BF16), w2a2_ref[...], preferred_element_type=F32)
    zw = -(w0_ref[...] + proj[:, :RWKV_W])
    softplus = jnp.maximum(zw, 0.0) + jnp.log(1.0 + jnp.exp(-jnp.abs(zw)))
    lw = -jnp.exp(-softplus - 0.5)
    a = _sigmoid(a0_ref[...] + proj[:, RWKV_W:])
    gate_ref[...] = jnp.dot(_sigmoid(gd).astype(BF16), g2_ref[...], preferred_element_type=F32)

    kk = k * kk_ref[...]
    ss = jnp.dot((kk * kk).astype(BF16), ones_ref[...], preferred_element_type=F32)
    kk = kk / jnp.maximum(jnp.sqrt(ss), 1e-12)
    k = k * (1.0 + (a - 1.0) * ka_ref[...])
    rk = jnp.dot((r * k * rk_ref[...]).astype(BF16), ones_ref[...], preferred_element_type=F32)
    bonus_ref[...] = rk * v

    half = tri_ref.shape[0]
    cums = []
    for r0 in range(0, tt, half):
        parts = _split3(lw[r0:r0 + half])
        cums.append(sum(jnp.dot(tri_ref[...], p, preferred_element_type=F32) for p in parts))
    cum = jnp.concatenate(cums, axis=0)
    e_cum = jnp.exp(cum)
    e_inv = jnp.exp(-cum)
    rt = r * e_cum
    at = -kk * jnp.exp(cum - lw)
    kt = k * e_inv
    bt = kk * a * e_inv

    nch = tt // CHUNK
    for c in range(nch):
        rows = slice(c * CHUNK, (c + 1) * CHUNK)
        ar_ref[0, c, 0:CHUNK, :] = at[rows].astype(BF16)
        ar_ref[0, c, CHUNK:, :] = rt[rows].astype(BF16)
        bk_ref[0, c, 0:CHUNK, :] = bt[rows].astype(BF16)
        bk_ref[0, c, CHUNK:, :] = kt[rows].astype(BF16)
        v_ref[0, c, :, :] = v[rows].astype(BF16)
        wc_ref[0, c, :, :] = jnp.broadcast_to(e_cum[(c + 1) * CHUNK - 1:(c + 1) * CHUNK], (SUBLANES, RWKV_W))


def _front_kernel(x0_ref, xn_ref, ng_ref, win_ref, inb_ref, dww_ref, dwb_ref, lng_ref, lnb_ref,
                  mu_ref, w0_ref, w2a2_ref, a0_ref, g2_ref, kk_ref, ka_ref, rk_ref, ones_ref, tri_ref,
                  c_ref, ar_ref, bk_ref, v_ref, wc_ref, gate_ref, bonus_ref, ext_ref, prc_ref,
                  *, tiles_per_seq, row_block):
    i = pl.program_id(0)
    tt = xn_ref.shape[0]

    def project(x_ref):
        h = _rms_norm(x_ref[...], ng_ref[...])
        return jnp.dot(h.astype(BF16), win_ref[...], preferred_element_type=F32)

    def make_current(p, conv_hist, shift_hist):
        z = p[:, :2 * CONV_W] + inb_ref[...]
        ext_ref[0, 0:CONV_HALO, :] = conv_hist
        ext_ref[0, CONV_HALO:, :] = z[:, :CONV_W] * _sigmoid(z[:, CONV_W:])
        _conv_shift_copies(ext_ref, tt)
        prc_ref[0:SUBLANES, :] = shift_hist
        prc_ref[SUBLANES:, :] = p[:, 2 * CONV_W:]

    @pl.when(i == 0)
    def _():
        make_current(project(x0_ref), jnp.zeros((CONV_HALO, CONV_W), F32), jnp.zeros((SUBLANES, RWKV_IN), F32))

    p_next = project(xn_ref)
    _conv_taps(ext_ref, dww_ref, dwb_ref, lng_ref, lnb_ref, c_ref, tt, row_block)
    _rwkv_prep_body(prc_ref, mu_ref, w0_ref, w2a2_ref, a0_ref, g2_ref, kk_ref, ka_ref, rk_ref, ones_ref, tri_ref,
                    ar_ref, bk_ref, v_ref, wc_ref, gate_ref, bonus_ref, tt)
    same_seq = (i + 1) % tiles_per_seq != 0
    make_current(p_next,
                 jnp.where(same_seq, ext_ref[0, tt:tt + CONV_HALO, :], 0.0),
                 jnp.where(same_seq, prc_ref[tt:tt + SUBLANES, :], 0.0))


def _front(x2, ng, w_in, inb, dww, dwb, lng, lnb, mu, w0, w2a2, a0, g2, k_k, k_a, r_k, ones, tri, bsz, seq, tt):
    n, d = x2.shape
    tps = seq // tt
    steps = n // tt
    nch = tt // CHUNK
    nc = seq // CHUNK
    kern = functools.partial(_front_kernel, tiles_per_seq=tps, row_block=64)
    hm = lambda rows: pl.BlockSpec((1, nch, rows, RWKV_W), lambda i: (i // tps, i % tps, 0, 0))
    hm_shape = lambda rows, dt: jax.ShapeDtypeStruct((bsz, nc, rows, RWKV_W), dt)
    consts = [ng, w_in, inb, dww, dwb, lng, lnb, mu, w0, w2a2, a0, g2, k_k, k_a, r_k, ones, tri]
    row = lambda w: pl.BlockSpec((tt, w), lambda i: (i, 0))
    return pl.pallas_call(
        kern,
        grid=(steps,),
        in_specs=[pl.BlockSpec((tt, d), lambda i: (0, 0)),
                  pl.BlockSpec((tt, d), lambda i: (jnp.minimum(i + 1, steps - 1), 0))]
                 + [_const_spec(c.shape) for c in consts],
        out_specs=[row(CONV_W), hm(2 * CHUNK), hm(2 * CHUNK), hm(CHUNK), hm(SUBLANES), row(RWKV_W), row(RWKV_W)],
        out_shape=[jax.ShapeDtypeStruct((n, CONV_W), BF16),
                   hm_shape(2 * CHUNK, BF16), hm_shape(2 * CHUNK, BF16), hm_shape(CHUNK, BF16), hm_shape(SUBLANES, F32),
                   jax.ShapeDtypeStruct((n, RWKV_W), F32), jax.ShapeDtypeStruct((n, RWKV_W), F32)],
        scratch_shapes=[pltpu.VMEM((SUBLANES, tt + CONV_HALO, CONV_W), F32),
                        pltpu.VMEM((tt + SUBLANES, RWKV_IN), F32)],
        compiler_params=_params(),
    )(x2, x2, *consts)


def _rwkv_core_kernel(ar_ref, bk_ref, v_ref, wc_ref, y_ref, ht_ref):
    @pl.when(pl.program_id(0) == 0)
    def _():
        ht_ref[...] = jnp.zeros_like(ht_ref)

    bsz, nchunks = ar_ref.shape[0], ar_ref.shape[1]
    gw = CORE_GROUP * RWKV_HEAD_DIM
    ngroups = RWKV_W // gw
    npar = bsz * ngroups
    row = lax.broadcasted_iota(jnp.int32, (CHUNK, gw), 0)
    col = lax.broadcasted_iota(jnp.int32, (CHUNK, gw), 1) % RWKV_HEAD_DIM
    strict = col < row
    incl = col <= row
    eye = (col == row).astype(F32)
    blk_r = lax.broadcasted_iota(jnp.int32, (gw, gw), 0) // RWKV_HEAD_DIM
    blk_c = lax.broadcasted_iota(jnp.int32, (gw, gw), 1) // RWKV_HEAD_DIM
    same_head = blk_r == blk_c
    head_of_lane = lax.broadcasted_iota(jnp.int32, (CHUNK, gw), 1) // RWKV_HEAD_DIM
    bmm = functools.partial(jnp.einsum, preferred_element_type=F32)

    def block_diag(z):
        tiled = jnp.concatenate([z] * CORE_GROUP, axis=1)
        return jnp.where(same_head[None], tiled, jnp.zeros_like(tiled))

    def grouped(ref, g0):
        return jnp.stack([ref[b, g, :, gi * gw:(gi + 1) * gw]
                          for g in range(g0, g0 + span) for b in range(bsz) for gi in range(ngroups)], axis=0)

    def state_free_part(g0, outs):
        ar = grouped(ar_ref, g0)
        bk = grouped(bk_ref, g0)
        vv = grouped(v_ref, g0)
        rhs = jnp.concatenate([block_diag(bk[:, 0:CHUNK]), block_diag(bk[:, CHUNK:])], axis=1)
        s = bmm('bmk,bnk->bmn', ar, rhs)
        yield
        a_ab = jnp.where(strict[None], s[:, 0:CHUNK, 0:gw], 0.0)
        a_ak = jnp.where(strict[None], s[:, 0:CHUNK, gw:], 0.0).astype(BF16)
        a_rb = jnp.where(incl[None], s[:, CHUNK:, 0:gw], 0.0).astype(BF16)
        a_rk = jnp.where(incl[None], s[:, CHUNK:, gw:], 0.0).astype(BF16)
        p = eye[None] + a_ab
        qb = a_ab.astype(BF16)
        q = bmm('bij,bjk->bik', qb, block_diag(qb))
        yield
        for _ in range(4):
            qb = q.astype(BF16)
            both = bmm('bij,bjk->bik', jnp.concatenate([qb, p.astype(BF16)], axis=1), block_diag(qb))
            yield
            q = both[:, 0:CHUNK]
            p = p + both[:, CHUNK:]
        p = p + bmm('bij,bjk->bik', p.astype(BF16), block_diag(q.astype(BF16)))
        yield
        bdv = block_diag(vv)
        av = bmm('bij,bjk->bik', a_ak, bdv)
        tb = p.astype(BF16)
        for j in range(span):
            sl = slice(j * npar, (j + 1) * npar)
            outs[g0 + j].update(ar=ar[sl], bk=bk[sl], vv=vv[sl], a_rb=a_rb[sl], a_rk=a_rk[sl], tb=tb[sl],
                                bdv=bdv[sl], av=av[sl])
        yield

    def recurrent_part(g0, outs):
        for g in range(g0, g0 + span):
            yield from recurrent_chunk(g, outs[g])

    def recurrent_chunk(g, c):
        ht = ht_ref[...]
        arh = bmm('bmk,bnk->bmn', c['ar'], block_diag(ht.astype(BF16)))
        yield
        u = bmm('bij,bjk->bik', c['tb'], block_diag((arh[:, 0:CHUNK] + c['av']).astype(BF16)))
        yield
        ub = u.astype(BF16)
        y = (arh[:, CHUNK:] + bmm('bij,bjk->bik', c['a_rb'], block_diag(ub))
             + bmm('bij,bjk->bik', c['a_rk'], c['bdv']))
        cross = bmm('bsi,bsj->bij', jnp.concatenate([ub, c['vv']], axis=1), c['bk'])
        yield
        upd = sum(jnp.where(head_of_lane == h, cross[:, h * RWKV_HEAD_DIM:(h + 1) * RWKV_HEAD_DIM], 0.0)
                  for h in range(CORE_GROUP))
        for b in range(bsz):
            for gi in range(ngroups):
                n = b * ngroups + gi
                lanes = slice(gi * gw, (gi + 1) * gw)
                ht_ref[n] = wc_ref[b, g, 0:1, lanes] * (ht[n] + upd[n])
                y_ref[b, g * CHUNK:(g + 1) * CHUNK, lanes] = y[n]

    def run_together(*gens):
        live = list(gens)
        while live:
            for gen in list(live):
                if next(gen, StopIteration) is StopIteration:
                    live.remove(gen)

    span = CORE_SPAN
    chunks = [dict() for _ in range(nchunks)]
    run_together(state_free_part(0, chunks))
    for g0 in range(span, nchunks, span):
        run_together(state_free_part(g0, chunks), recurrent_part(g0 - span, chunks))
    run_together(recurrent_part(nchunks - span, chunks))


def _rwkv_core(ar, bk, v, wc):
    bsz, nc = ar.shape[:2]
    g = CORE_CHUNKS
    gw = CORE_GROUP * RWKV_HEAD_DIM
    blk = lambda rows: pl.BlockSpec((bsz, g, rows, RWKV_W), lambda i: (0, i, 0, 0))
    y = pl.pallas_call(
        _rwkv_core_kernel,
        grid=(nc // g,),
        in_specs=[blk(2 * CHUNK), blk(2 * CHUNK), blk(CHUNK), blk(SUBLANES)],
        out_specs=pl.BlockSpec((bsz, g * CHUNK, RWKV_W), lambda i: (0, i, 0)),
        out_shape=jax.ShapeDtypeStruct((bsz, nc * CHUNK, RWKV_W), F32),
        scratch_shapes=[pltpu.VMEM((bsz * (RWKV_W // gw), RWKV_HEAD_DIM, gw), F32)],
        compiler_params=_params(),
    )(ar, bk, v, wc)
    return y.reshape(bsz * nc * CHUNK, RWKV_W)


def _mix_ffn_kernel(x_ref, c_ref, y_ref, gate_ref, bonus_ref, lng_ref, lnb_ref, ones_ref, w_ref,
                    g_ref, wup_ref, cw_ref, cb_ref, wdn_ref, o_ref, hist_ref, gs_ref, *, tiles_per_seq):
    i = pl.program_id(0)
    tt = x_ref.shape[0]

    @pl.when(i == 0)
    def _():
        hist_ref[...] = jnp.zeros_like(hist_ref)

    inv_n = 1.0 / RWKV_HEAD_DIM
    y = y_ref[...]
    half = ones_ref.shape[0]

    def head_sum(z):
        zb = z.astype(BF16)
        return jnp.concatenate([jnp.dot(zb[:, c0:c0 + half], ones_ref[...], preferred_element_type=F32)
                                for c0 in range(0, z.shape[1], half)], axis=1)

    yc = y - head_sum(y) * inv_n
    var = head_sum(yc * yc) * inv_n
    yn = yc * lax.rsqrt(var + RWKV_GN_EPS) * lng_ref[...] + lnb_ref[...]
    yo = ((yn + bonus_ref[...]) * gate_ref[...]).astype(BF16)
    mix = (jnp.dot(c_ref[...], w_ref[0:CONV_W, :], preferred_element_type=F32)
           + jnp.dot(yo, w_ref[CONV_W:, :], preferred_element_type=F32))
    x1 = x_ref[...] + mix
    hist = jnp.where(i % tiles_per_seq != 0, hist_ref[...], 0.0)
    o_ref[...] = _ffn_tile(x1, hist, g_ref, wup_ref, cw_ref, cb_ref, wdn_ref, gs_ref)
    hist_ref[...] = x1[tt - SUBLANES:, :]


def _mix_ffn(x2, c, y, gate, bonus, lng, lnb, ones, w_out, g, wup, cw, cb, wdn, layer, seq, tt):
    n, d = x2.shape
    row = lambda w: pl.BlockSpec((tt, w), lambda i: (i, 0))
    consts = [lng, lnb, ones, w_out]
    ffn_consts = [g, wup, cw, cb, wdn]
    kern = functools.partial(_mix_ffn_kernel, tiles_per_seq=seq // tt)
    return pl.pallas_call(
        kern,
        grid=(n // tt,),
        in_specs=[row(d), row(CONV_W), row(RWKV_W), row(RWKV_W), row(RWKV_W)]
                 + [_const_spec(c_.shape) for c_ in consts] + [_layer_spec(c_, layer) for c_ in ffn_consts],
        out_specs=row(d),
        out_shape=jax.ShapeDtypeStruct((n, d), F32),
        scratch_shapes=[pltpu.VMEM((SUBLANES, d), F32), pltpu.VMEM((tt + SUBLANES, D_FF), F32)],
        compiler_params=_params(),
    )(x2, c, y, gate, bonus, *consts, *ffn_consts)


def _ffn_tile(x, hist, g_ref, wup_ref, cw_ref, cb_ref, wdn_ref, gs_ref):
    tt = x.shape[0]
    xe = jnp.concatenate([hist, x], axis=0)
    h = _rms_norm(xe, g_ref[...]).astype(BF16)
    u = jnp.dot(h, wup_ref[...], preferred_element_type=F32)
    gs_ref[...] = u[:, :D_FF]
    gate = cb_ref[...]
    for j in range(FFN_CONV_K):
        gate = gate + cw_ref[j:j + 1, :] * gs_ref[pl.ds(SUBLANES - (FFN_CONV_K - 1) + j, tt), :]
    act = (gate * _sigmoid(gate) * u[SUBLANES:, D_FF:]).astype(BF16)
    return x + jnp.dot(act, wdn_ref[...], preferred_element_type=F32)


def _ffn_kernel(x_ref, halo_ref, g_ref, wup_ref, cw_ref, cb_ref, wdn_ref, o_ref, gs_ref, *, tiles_per_seq):
    keep = (pl.program_id(0) % tiles_per_seq != 0).astype(F32)
    o_ref[...] = _ffn_tile(x_ref[...], halo_ref[...] * keep, g_ref, wup_ref, cw_ref, cb_ref, wdn_ref, gs_ref)


def _layer_spec(stacked, layer):
    tail = stacked.shape[1:]
    return pl.BlockSpec((None,) + tail, lambda *_: (layer,) + (0,) * len(tail), pipeline_mode=pl.Buffered(1))


def _ffn(x2, g, wup, cw, cb, wdn, layer, seq, tt):
    n, d = x2.shape
    tps = seq // tt
    hb = tt // SUBLANES
    kern = functools.partial(_ffn_kernel, tiles_per_seq=tps)
    consts = [g, wup, cw, cb, wdn]
    return pl.pallas_call(
        kern,
        grid=(n // tt,),
        in_specs=[pl.BlockSpec((tt, d), lambda i: (i, 0)),
                  pl.BlockSpec((SUBLANES, d), lambda i: (jnp.maximum(i * hb - 1, 0), 0))]
                 + [_layer_spec(c, layer) for c in consts],
        out_specs=pl.BlockSpec((tt, d), lambda i: (i, 0)),
        out_shape=jax.ShapeDtypeStruct((n, d), F32),
        scratch_shapes=[pltpu.VMEM((tt + SUBLANES, D_FF), F32)],
        compiler_params=_params(),
    )(x2, x2, *consts)


def _rope_table_kernel(freq_ref, pos_ref, cs_ref):
    half = ROT_DIM // 2
    pos = pos_ref[...]
    for f in range(half):
        ang = pos * freq_ref[f]
        cs_ref[f] = jnp.cos(ang)
        cs_ref[half + f] = jnp.sin(ang)


def _rope_table(inv_freq, pos_dense):
    rows, lanes = pos_dense.shape
    return pl.pallas_call(
        _rope_table_kernel,
        in_specs=[pl.BlockSpec(memory_space=pltpu.SMEM), pl.BlockSpec((rows, lanes), lambda: (0, 0))],
        out_specs=pl.BlockSpec((ROT_DIM, rows, lanes), lambda: (0, 0, 0)),
        out_shape=jax.ShapeDtypeStruct((ROT_DIM, rows, lanes), F32),
    )(inv_freq, pos_dense)


def _qkv_kernel(x_ref, cs_ref, g_ref, w_ref, b_ref, qg_ref, kg_ref, expand_ref, ones_ref,
                q_ref, k_ref, v_ref):
    qd = N_HEADS * HEAD_DIM
    kd = N_KV_HEADS * HEAD_DIM
    h = _rms_norm(x_ref[...], g_ref[...]).astype(BF16)

    def project(c0, width):
        return jnp.dot(h, w_ref[:, c0:c0 + width], preferred_element_type=F32) + b_ref[:, c0:c0 + width]

    lanes = expand_ref.shape[1] // 3
    cs_hi, cs_lo, _ = _split3(cs_ref[...])
    tab = (jnp.dot(cs_hi, expand_ref[...], preferred_element_type=F32)
           + jnp.dot(cs_lo, expand_ref[...], preferred_element_type=F32))
    dim = lax.broadcasted_iota(jnp.int32, (1, lanes), 1) % HEAD_DIM
    cos = tab[:, :lanes] + (dim >= ROT_DIM).astype(F32)
    sin_a = tab[:, lanes:2 * lanes]
    sin_b = tab[:, 2 * lanes:]
    half = ROT_DIM // 2

    def norm_rope(c0, gain, scale, dst_ref, head0):
        z = project(c0, blk)
        yield
        ss = jnp.dot((z * z).astype(BF16), ones_ref[...], preferred_element_type=F32)
        yield
        z = z * lax.rsqrt(ss * (1.0 / HEAD_DIM) + NORM_EPS) * gain
        for j, c0 in enumerate(range(0, z.shape[1], lanes)):
            zz = z[:, c0:c0 + lanes]
            rot = (zz * cos + pltpu.roll(zz, lanes - half, 1) * sin_a + pltpu.roll(zz, half, 1) * sin_b) * scale
            dst_ref[0, head0 + 2 * j] = rot[:, :HEAD_DIM].astype(BF16)
            dst_ref[0, head0 + 2 * j + 1] = rot[:, HEAD_DIM:].astype(BF16)
            yield

    def values():
        vt = project(qd + kd, kd).T
        yield
        for j in range(N_KV_HEADS):
            v_ref[0, j] = vt[j * HEAD_DIM:(j + 1) * HEAD_DIM, :].astype(BF16)
        yield

    blk = 4 * HEAD_DIM
    live = [norm_rope(c0, qg_ref[...], HEAD_DIM ** -0.5 * LOG2_E, q_ref, c0 // HEAD_DIM) for c0 in range(0, qd, blk)]
    live += [norm_rope(qd, kg_ref[...], 1.0, k_ref, 0), values()]
    while live:
        live = [u for u in live if next(u, StopIteration) is not StopIteration]


def _qkv(x2, cs, g, w, b, qg, kg, expand, ones, bsz, seq, tt):
    n, d = x2.shape
    tps = seq // tt
    consts = [g, w, b, qg, kg, expand, ones]
    hm = lambda nh: pl.BlockSpec((1, nh, tt, HEAD_DIM), lambda i: (i // tps, 0, i % tps, 0))
    return pl.pallas_call(
        _qkv_kernel,
        grid=(n // tt,),
        in_specs=[pl.BlockSpec((tt, d), lambda i: (i, 0)), pl.BlockSpec((tt, ROT_DIM), lambda i: (i, 0))]
                 + [_const_spec(c.shape) for c in consts],
        out_specs=[hm(N_HEADS), hm(N_KV_HEADS),
                   pl.BlockSpec((1, N_KV_HEADS, HEAD_DIM, tt), lambda i: (i // tps, 0, 0, i % tps))],
        out_shape=[jax.ShapeDtypeStruct((bsz, N_HEADS, seq, HEAD_DIM), BF16),
                   jax.ShapeDtypeStruct((bsz, N_KV_HEADS, seq, HEAD_DIM), BF16),
                   jax.ShapeDtypeStruct((bsz, N_KV_HEADS, HEAD_DIM, seq), BF16)],
        compiler_params=_params(),
    )(x2, cs, *consts)


def _attn_kernel(sink_ref, q_ref, kc_ref, kp_ref, vc_ref, vp_ref, x_ref, w_ref, b_ref, out_ref, *, q_blocks):
    seq_start = pl.program_id(1) == 0
    blk = WINDOW
    cols = GROUP * blk
    c = lax.broadcasted_iota(jnp.int32, (2 * blk, cols), 0)
    t = lax.broadcasted_iota(jnp.int32, (2 * blk, cols), 1) % blk
    d = c - t
    band = (d >= 1) & (d <= blk)
    head_of_lane = lax.broadcasted_iota(jnp.int32, (1, cols), 1) // blk
    def unit(qb, g, outs):
        valid = band & ((c >= blk) | jnp.logical_not(seq_start)) if qb == 0 else band
        q = q_ref[0, g * GROUP:(g + 1) * GROUP, qb * blk:(qb + 1) * blk, :].reshape(cols, HEAD_DIM)
        if qb == 0:
            kprev, vprev = kp_ref[0, g], vp_ref[0, g]
        else:
            kprev = kc_ref[0, g, (qb - 1) * blk:qb * blk, :]
            vprev = vc_ref[0, g, :, (qb - 1) * blk:qb * blk]
        k = jnp.concatenate([kprev, kc_ref[0, g, qb * blk:(qb + 1) * blk, :]], axis=0)
        vt = jnp.concatenate([vprev, vc_ref[0, g, :, qb * blk:(qb + 1) * blk]], axis=1)
        s = lax.dot_general(k, q, NT_DIMS, preferred_element_type=F32)
        yield
        s = jnp.where(valid, s, -jnp.inf)
        sink = jnp.zeros((1, cols), F32)
        for j in range(GROUP):
            sink = jnp.where(head_of_lane == j, sink_ref[g * GROUP + j] * LOG2_E, sink)
        m = jnp.maximum(jnp.max(s, axis=0, keepdims=True), sink)
        p = jnp.exp2(s - m)
        den = jnp.sum(p, axis=0, keepdims=True) + jnp.exp2(sink - m)
        yield
        ot = jnp.dot(vt, p.astype(BF16), preferred_element_type=F32) / den
        yield
        o = ot.T
        outs[g] = [o[j * blk:(j + 1) * blk] for j in range(GROUP)]
        yield

    blocks = []
    for qb0 in range(0, q_blocks, ATTN_TOGETHER):
        qbs = range(qb0, min(qb0 + ATTN_TOGETHER, q_blocks))
        outs = {qb: [None] * N_KV_HEADS for qb in qbs}
        live = [unit(qb, g, outs[qb]) for qb in qbs for g in range(N_KV_HEADS)]
        while live:
            live = [u for u in live if next(u, StopIteration) is not StopIteration]
        for qb in qbs:
            blocks.append(jnp.concatenate([o for group in outs[qb] for o in group], axis=1).astype(BF16))
    o_all = jnp.concatenate(blocks, axis=0)
    out_ref[...] = x_ref[...] + jnp.dot(o_all, w_ref[...], preferred_element_type=F32) + b_ref[...]


def _attention(sinks, q, k, v, x2, w_o, b_o, q_blocks):
    bsz, _, seq, _ = q.shape
    d = x2.shape[1]
    tq = q_blocks * WINDOW
    steps = seq // tq
    kern = functools.partial(_attn_kernel, q_blocks=q_blocks)
    cur = lambda nh: pl.BlockSpec((1, nh, tq, HEAD_DIM), lambda b, i: (b, 0, i, 0))
    prev = pl.BlockSpec((1, N_KV_HEADS, WINDOW, HEAD_DIM), lambda b, i: (b, 0, jnp.maximum(i * q_blocks - 1, 0), 0))
    rows = pl.BlockSpec((tq, d), lambda b, i: (b * steps + i, 0))
    vcur = pl.BlockSpec((1, N_KV_HEADS, HEAD_DIM, tq), lambda b, i: (b, 0, 0, i))
    vprev = pl.BlockSpec((1, N_KV_HEADS, HEAD_DIM, WINDOW), lambda b, i: (b, 0, 0, jnp.maximum(i * q_blocks - 1, 0)))
    return pl.pallas_call(
        kern,
        grid=(bsz, steps),
        in_specs=[pl.BlockSpec(memory_space=pltpu.SMEM), cur(N_HEADS), cur(N_KV_HEADS), prev, vcur, vprev,
                  rows, _const_spec(w_o.shape), _const_spec(b_o.shape)],
        out_specs=rows,
        out_shape=jax.ShapeDtypeStruct((bsz * seq, d), F32),
        compiler_params=_params(2),
    )(sinks, q, k, k, v, v, x2, w_o, b_o)


def _block_ones(width, group):
    idx = np.arange(width) // group
    return jnp.asarray(idx[:, None] == idx[None, :], BF16)


def _chunk_tril(rows):
    idx = np.arange(rows)
    same = (idx[:, None] // CHUNK) == (idx[None, :] // CHUNK)
    return jnp.asarray(same & (idx[None, :] <= idx[:, None]), BF16)


def _rope_expand(lanes=128):
    half = ROT_DIM // 2
    dim = np.arange(lanes) % HEAD_DIM
    f = np.arange(half)[:, None]
    hit = (dim[None, :] % half == f)
    zero = np.zeros((half, lanes))
    cos_rows = np.concatenate([hit & (dim < ROT_DIM)[None, :], zero, zero], axis=1)
    sin_rows = np.concatenate([zero, -1.0 * (hit & (dim < half)[None, :]),
                               hit & ((dim >= half) & (dim < ROT_DIM))[None, :]], axis=1)
    return jnp.asarray(np.concatenate([cos_rows, sin_rows], axis=0), BF16)


def _row(v):
    return v.reshape(1, -1)


def kernel(x, positions, ab_norm_g, ab_w_in, conv_in_b, conv_dw_w, conv_dw_b, conv_ln_g, conv_ln_b, rwkv_mu, rwkv_w0, rwkv_w2, rwkv_a0, rwkv_a2, rwkv_g2, rwkv_k_k, rwkv_k_a, rwkv_r_k, rwkv_ln_g, rwkv_ln_b, ab_w_out, attn_norm_g, attn_w_qkv, attn_b_qkv, attn_q_norm_g, attn_k_norm_g, attn_sinks, attn_w_o, attn_b_o, ffn_norm_g, ffn_w_up, ffn_conv_w, ffn_conv_b, ffn_w_down):
    bsz, seq, d = x.shape
    depth = ffn_norm_g.shape[0]
    n = bsz * seq
    tt = 512
    ffn_tt = 512
    qkv_tt = 1024
    x2 = x.reshape(n, d)
    ones_rwkv = _block_ones(RWKV_W, RWKV_HEAD_DIM)
    ones_quad = _block_ones(4 * HEAD_DIM, HEAD_DIM)
    tri = _chunk_tril(256)
    rope_expand = _rope_expand()
    ffn_g = ffn_norm_g[:, None, :]
    ffn_cb = ffn_conv_b[:, None, :]
    ffn_up = ffn_w_up.astype(BF16)
    ffn_down = ffn_w_down.astype(BF16)

    for layer in range(depth):
        i = layer // 2
        if layer % 2 == 0:
            zeros = jnp.zeros((LORA_DECAY, RWKV_W), F32)
            w2a2 = jnp.concatenate([jnp.concatenate([rwkv_w2[i], zeros], axis=1),
                                    jnp.concatenate([zeros, rwkv_a2[i]], axis=1)], axis=0).astype(BF16)
            c, ar, bk, vv, wc, gate, bonus = _front(
                x2, _row(ab_norm_g[i]), ab_w_in[i].astype(BF16),
                _row(conv_in_b[i]), conv_dw_w[i], _row(conv_dw_b[i]), _row(conv_ln_g[i]), _row(conv_ln_b[i]),
                _row(rwkv_mu[i]), _row(rwkv_w0[i]), w2a2, _row(rwkv_a0[i]), rwkv_g2[i].astype(BF16),
                _row(rwkv_k_k[i]), _row(rwkv_k_a[i]), _row(rwkv_r_k[i]), ones_rwkv, tri, bsz, seq, tt)
            y = _rwkv_core(ar, bk, vv, wc)
            x2 = _mix_ffn(x2, c, y, gate, bonus, _row(rwkv_ln_g[i]), _row(rwkv_ln_b[i]), ones_quad,
                          ab_w_out[i].astype(BF16), ffn_g, ffn_up, ffn_conv_w, ffn_cb, ffn_down, layer, seq, ffn_tt)
            continue
        else:
            half = ROT_DIM // 2
            inv_freq = ROPE_THETA ** (-(jnp.arange(half, dtype=F32) * 2.0) / ROT_DIM)
            lanes = 128
            cs = _rope_table(inv_freq, positions.astype(F32).reshape(n // lanes, lanes))
            cs = cs.reshape(ROT_DIM, n).T
            q, k, v = _qkv(x2, cs, _row(attn_norm_g[i]), attn_w_qkv[i].astype(BF16), _row(attn_b_qkv[i]),
                           _row(jnp.tile(attn_q_norm_g[i], 4)), _row(jnp.tile(attn_k_norm_g[i], 4)),
                           rope_expand, ones_quad, bsz, seq, qkv_tt)
            x2 = _attention(attn_sinks[i], q, k, v, x2, attn_w_o[i].astype(BF16), _row(attn_b_o[i]), 8)
        x2 = _ffn(x2, ffn_g, ffn_up, ffn_conv_w, ffn_cb, ffn_down, layer, seq, ffn_tt)
    return x2.reshape(bsz, seq, d)
```

```python
import functools

import jax
import jax.numpy as jnp
import numpy as np
from jax import lax
from jax.experimental import pallas as pl
from jax.experimental.pallas import tpu as pltpu

F32 = jnp.float32
BF16 = jnp.bfloat16

CONV_W = 512
CONV_K = 31
CONV_LN_EPS = 1e-5
RWKV_HEADS = 8
RWKV_HEAD_DIM = 64
RWKV_W = RWKV_HEADS * RWKV_HEAD_DIM
LORA_DECAY = 64
LORA_ICLR = 64
LORA_GATE = 128
RWKV_GN_EPS = RWKV_HEAD_DIM * 1e-5
RWKV_IN = 3 * RWKV_W + LORA_DECAY + LORA_ICLR + LORA_GATE
HEAD_DIM = 64
N_HEADS = 16
N_KV_HEADS = 4
GROUP = N_HEADS // N_KV_HEADS
WINDOW = 128
ROT_DIM = 16
ROPE_THETA = 500000.0
D_FF = 2816
FFN_CONV_K = 3
NORM_EPS = 1e-6
LOG2_E = 1.4426950408889634

CHUNK = 64
CORE_CHUNKS = 16
CORE_SPAN = 2
CORE_GROUP = 4
SUBLANES = 8
CONV_HALO = 32
ATTN_TOGETHER = 4
V7X_VMEM_LIMIT = 56 * 1024 * 1024

NT_DIMS = (((1,), (1,)), ((), ()))


def _params(n_axes=1):
    return pltpu.CompilerParams(dimension_semantics=("arbitrary",) * n_axes,
                                vmem_limit_bytes=V7X_VMEM_LIMIT)


def _const_spec(shape):
    nd = len(shape)
    return pl.BlockSpec(shape, lambda *_: (0,) * nd, pipeline_mode=pl.Buffered(1))


def _rms_norm(x, g):
    return x * lax.rsqrt(jnp.mean(x * x, axis=-1, keepdims=True) + NORM_EPS) * g


def _sigmoid(z):
    return 1.0 / (1.0 + jnp.exp(-z))


def _split3(z):
    hi = z.astype(BF16)
    r1 = z - hi.astype(F32)
    mid = r1.astype(BF16)
    lo = (r1 - mid.astype(F32)).astype(BF16)
    return hi, mid, lo


def _conv_shift_copies(ext_ref, tt):
    span = tt + CONV_HALO - SUBLANES
    for s in range(1, SUBLANES):
        ext_ref[s, 0:span, :] = ext_ref[0, pl.ds(s, span), :]


def _conv_taps(ext_ref, dww_ref, dwb_ref, lng_ref, lnb_ref, c_ref, tt, row_block):
    first = CONV_HALO - (CONV_K - 1)
    for r0 in range(0, tt, row_block):
        acc = jnp.broadcast_to(dwb_ref[...], (row_block, CONV_W))
        for s in range(SUBLANES):
            taps = [j for j in range(CONV_K) if (first + j) % SUBLANES == s]
            lo = first + taps[0] - s
            hi = first + taps[-1] - s
            win = ext_ref[s, pl.ds(r0 + lo, row_block + hi - lo), :]
            for j in taps:
                a = first + j - s - lo
                acc = acc + dww_ref[j:j + 1, :] * win[a:a + row_block]
        mu = jnp.mean(acc, axis=-1, keepdims=True)
        xc = acc - mu
        var = jnp.mean(xc * xc, axis=-1, keepdims=True)
        z = xc * lax.rsqrt(var + CONV_LN_EPS) * lng_ref[...] + lnb_ref[...]
        c_ref[r0:r0 + row_block, :] = (z * _sigmoid(z)).astype(c_ref.dtype)


def _convmod(pc, inb, dww, dwb, lng, lnb, seq, tt):
    n = pc.shape[0]
    tps = seq // tt
    hb = tt // CONV_HALO
    kern = functools.partial(_convmod_kernel, tiles_per_seq=tps, row_block=64)
    return pl.pallas_call(
        kern,
        grid=(n // tt,),
        in_specs=[pl.BlockSpec((tt, 2 * CONV_W), lambda i: (i, 0)),
                  pl.BlockSpec((CONV_HALO, 2 * CONV_W), lambda i: (jnp.maximum(i * hb - 1, 0), 0)),
                  _const_spec(inb.shape), _const_spec(dww.shape), _const_spec(dwb.shape),
                  _const_spec(lng.shape), _const_spec(lnb.shape)],
        out_specs=pl.BlockSpec((tt, CONV_W), lambda i: (i, 0)),
        out_shape=jax.ShapeDtypeStruct((n, CONV_W), BF16),
        scratch_shapes=[pltpu.VMEM((SUBLANES, tt + CONV_HALO, CONV_W), F32)],
        compiler_params=_params(),
    )(pc, pc, inb, dww, dwb, lng, lnb)


def _rwkv_prep_body(ext_ref, mu_ref, w0_ref, w2a2_ref, a0_ref, g2_ref, kk_ref, ka_ref, rk_ref, ones_ref, tri_ref,
                    ar_ref, bk_ref, v_ref, wc_ref, gate_ref, bonus_ref, tt):
    rw = ext_ref[SUBLANES:, :]
    rw = rw + (ext_ref[pl.ds(SUBLANES - 1, tt), :] - rw) * mu_ref[...]

    r = rw[:, 0:RWKV_W]
    k = rw[:, RWKV_W:2 * RWKV_W]
    v = rw[:, 2 * RWKV_W:3 * RWKV_W]
    wa = rw[:, 3 * RWKV_W:3 * RWKV_W + LORA_DECAY + LORA_ICLR]
    gd = rw[:, 3 * RWKV_W + LORA_DECAY + LORA_ICLR:]

    lane = lax.broadcasted_iota(jnp.int32, wa.shape, 1)
    z = jnp.where(lane < LORA_DECAY, jnp.tanh(wa), wa)
    proj = jnp.dot(z.astype(BF16), w2a2_ref[...], preferred_element_type=F32)
    zw = -(w0_ref[...] + proj[:, :RWKV_W])
    softplus = jnp.maximum(zw, 0.0) + jnp.log(1.0 + jnp.exp(-jnp.abs(zw)))
    lw = -jnp.exp(-softplus - 0.5)
    a = _sigmoid(a0_ref[...] + proj[:, RWKV_W:])
    gate_ref[...] = jnp.dot(_sigmoid(gd).astype(BF16), g2_ref[...], preferred_element_type=F32)

    kk = k * kk_ref[...]
    ss = jnp.dot((kk * kk).astype(BF16), ones_ref[...], preferred_element_type=F32)
    kk = kk / jnp.maximum(jnp.sqrt(ss), 1e-12)
    k = k * (1.0 + (a - 1.0) * ka_ref[...])
    rk = jnp.dot((r * k * rk_ref[...]).astype(BF16), ones_ref[...], preferred_element_type=F32)
    bonus_ref[...] = rk * v

    half = tri_ref.shape[0]
    cums = []
    for r0 in range(0, tt, half):
        parts = _split3(lw[r0:r0 + half])
        cums.append(sum(jnp.dot(tri_ref[...], p, preferred_element_type=F32) for p in parts))
    cum = jnp.concatenate(cums, axis=0)
    e_cum = jnp.exp(cum)
    e_inv = jnp.exp(-cum)
    rt = r * e_cum
    at = -kk * jnp.exp(cum - lw)
    kt = k * e_inv
    bt = kk * a * e_inv

    nch = tt // CHUNK
    for c in range(nch):
        rows = slice(c * CHUNK, (c + 1) * CHUNK)
        ar_ref[0, c, 0:CHUNK, :] = at[rows].astype(BF16)
        ar_ref[0, c, CHUNK:, :] = rt[rows].astype(BF16)
        bk_ref[0, c, 0:CHUNK, :] = bt[rows].astype(BF16)
        bk_ref[0, c, CHUNK:, :] = kt[rows].astype(BF16)
        v_ref[0, c, :, :] = v[rows].astype(BF16)
        wc_ref[0, c, :, :] = jnp.broadcast_to(e_cum[(c + 1) * CHUNK - 1:(c + 1) * CHUNK], (SUBLANES, RWKV_W))


def _front_kernel(x0_ref, xn_ref, ng_ref, win_ref, inb_ref, dww_ref, dwb_ref, lng_ref, lnb_ref,
                  mu_ref, w0_ref, w2a2_ref, a0_ref, g2_ref, kk_ref, ka_ref, rk_ref, ones_ref, tri_ref,
                  c_ref, ar_ref, bk_ref, v_ref, wc_ref, gate_ref, bonus_ref, ext_ref, prc_ref,
                  *, tiles_per_seq, row_block):
    i = pl.program_id(0)
    tt = xn_ref.shape[0]

    def project(x_ref):
        h = _rms_norm(x_ref[...], ng_ref[...])
        return jnp.dot(h.astype(BF16), win_ref[...], preferred_element_type=F32)

    def make_current(p, conv_hist, shift_hist):
        z = p[:, :2 * CONV_W] + inb_ref[...]
        ext_ref[0, 0:CONV_HALO, :] = conv_hist
        ext_ref[0, CONV_HALO:, :] = z[:, :CONV_W] * _sigmoid(z[:, CONV_W:])
        _conv_shift_copies(ext_ref, tt)
        prc_ref[0:SUBLANES, :] = shift_hist
        prc_ref[SUBLANES:, :] = p[:, 2 * CONV_W:]

    @pl.when(i == 0)
    def _():
        make_current(project(x0_ref), jnp.zeros((CONV_HALO, CONV_W), F32), jnp.zeros((SUBLANES, RWKV_IN), F32))

    p_next = project(xn_ref)
    _conv_taps(ext_ref, dww_ref, dwb_ref, lng_ref, lnb_ref, c_ref, tt, row_block)
    _rwkv_prep_body(prc_ref, mu_ref, w0_ref, w2a2_ref, a0_ref, g2_ref, kk_ref, ka_ref, rk_ref, ones_ref, tri_ref,
                    ar_ref, bk_ref, v_ref, wc_ref, gate_ref, bonus_ref, tt)
    same_seq = (i + 1) % tiles_per_seq != 0
    make_current(p_next,
                 jnp.where(same_seq, ext_ref[0, tt:tt + CONV_HALO, :], 0.0),
                 jnp.where(same_seq, prc_ref[tt:tt + SUBLANES, :], 0.0))


def _front(x2, ng, w_in, inb, dww, dwb, lng, lnb, mu, w0, w2a2, a0, g2, k_k, k_a, r_k, ones, tri, bsz, seq, tt):
    n, d = x2.shape
    tps = seq // tt
    steps = n // tt
    nch = tt // CHUNK
    nc = seq // CHUNK
    kern = functools.partial(_front_kernel, tiles_per_seq=tps, row_block=64)
    hm = lambda rows: pl.BlockSpec((1, nch, rows, RWKV_W), lambda i: (i // tps, i % tps, 0, 0))
    hm_shape = lambda rows, dt: jax.ShapeDtypeStruct((bsz, nc, rows, RWKV_W), dt)
    consts = [ng, w_in, inb, dww, dwb, lng, lnb, mu, w0, w2a2, a0, g2, k_k, k_a, r_k, ones, tri]
    row = lambda w: pl.BlockSpec((tt, w), lambda i: (i, 0))
    return pl.pallas_call(
        kern,
        grid=(steps,),
        in_specs=[pl.BlockSpec((tt, d), lambda i: (0, 0)),
                  pl.BlockSpec((tt, d), lambda i: (jnp.minimum(i + 1, steps - 1), 0))]
                 + [_const_spec(c.shape) for c in consts],
        out_specs=[row(CONV_W), hm(2 * CHUNK), hm(2 * CHUNK), hm(CHUNK), hm(SUBLANES), row(RWKV_W), row(RWKV_W)],
        out_shape=[jax.ShapeDtypeStruct((n, CONV_W), BF16),
                   hm_shape(2 * CHUNK, BF16), hm_shape(2 * CHUNK, BF16), hm_shape(CHUNK, BF16), hm_shape(SUBLANES, F32),
                   jax.ShapeDtypeStruct((n, RWKV_W), F32), jax.ShapeDtypeStruct((n, RWKV_W), F32)],
        scratch_shapes=[pltpu.VMEM((SUBLANES, tt + CONV_HALO, CONV_W), F32),
                        pltpu.VMEM((tt + SUBLANES, RWKV_IN), F32)],
        compiler_params=_params(),
    )(x2, x2, *consts)


def _rwkv_core_kernel(ar_ref, bk_ref, v_ref, wc_ref, y_ref, ht_ref):
    @pl.when(pl.program_id(0) == 0)
    def _():
        ht_ref[...] = jnp.zeros_like(ht_ref)

    bsz, nchunks = ar_ref.shape[0], ar_ref.shape[1]
    gw = CORE_GROUP * RWKV_HEAD_DIM
    ngroups = RWKV_W // gw
    npar = bsz * ngroups
    row = lax.broadcasted_iota(jnp.int32, (CHUNK, gw), 0)
    col = lax.broadcasted_iota(jnp.int32, (CHUNK, gw), 1) % RWKV_HEAD_DIM
    strict = col < row
    incl = col <= row
    eye = (col == row).astype(F32)
    blk_r = lax.broadcasted_iota(jnp.int32, (gw, gw), 0) // RWKV_HEAD_DIM
    blk_c = lax.broadcasted_iota(jnp.int32, (gw, gw), 1) // RWKV_HEAD_DIM
    same_head = blk_r == blk_c
    head_of_lane = lax.broadcasted_iota(jnp.int32, (CHUNK, gw), 1) // RWKV_HEAD_DIM
    bmm = functools.partial(jnp.einsum, preferred_element_type=F32)

    def block_diag(z):
        tiled = jnp.concatenate([z] * CORE_GROUP, axis=1)
        return jnp.where(same_head[None], tiled, jnp.zeros_like(tiled))

    def grouped(ref, g0):
        return jnp.stack([ref[b, g, :, gi * gw:(gi + 1) * gw]
                          for g in range(g0, g0 + span) for b in range(bsz) for gi in range(ngroups)], axis=0)

    def state_free_part(g0, outs):
        ar = grouped(ar_ref, g0)
        bk = grouped(bk_ref, g0)
        vv = grouped(v_ref, g0)
        rhs = jnp.concatenate([block_diag(bk[:, 0:CHUNK]), block_diag(bk[:, CHUNK:])], axis=1)
        s = bmm('bmk,bnk->bmn', ar, rhs)
        yield
        a_ab = jnp.where(strict[None], s[:, 0:CHUNK, 0:gw], 0.0)
        a_ak = jnp.where(strict[None], s[:, 0:CHUNK, gw:], 0.0).astype(BF16)
        a_rb = jnp.where(incl[None], s[:, CHUNK:, 0:gw], 0.0).astype(BF16)
        a_rk = jnp.where(incl[None], s[:, CHUNK:, gw:], 0.0).astype(BF16)
        p = eye[None] + a_ab
        qb = a_ab.astype(BF16)
        q = bmm('bij,bjk->bik', qb, block_diag(qb))
        yield
        for _ in range(4):
            qb = q.astype(BF16)
            both = bmm('bij,bjk->bik', jnp.concatenate([qb, p.astype(BF16)], axis=1), block_diag(qb))
            yield
            q = both[:, 0:CHUNK]
            p = p + both[:, CHUNK:]
        p = p + bmm('bij,bjk->bik', p.astype(BF16), block_diag(q.astype(BF16)))
        yield
        both = bmm('bij,bjk->bik', jnp.concatenate([a_ak, a_rk], axis=1), block_diag(vv))
        tb = p.astype(BF16)
        for j in range(span):
            sl = slice(j * npar, (j + 1) * npar)
            outs[g0 + j].update(ar=ar[sl], bk=bk[sl], vv=vv[sl], a_rb=a_rb[sl], tb=tb[sl],
                                av=both[sl, 0:CHUNK], rkv=both[sl, CHUNK:])
        yield

    def recurrent_part(g0, outs):
        for g in range(g0, g0 + span):
            yield from recurrent_chunk(g, outs[g])

    def recurrent_chunk(g, c):
        ht = ht_ref[...]
        arh = bmm('bmk,bnk->bmn', c['ar'], block_diag(ht.astype(BF16)))
        yield
        u = bmm('bij,bjk->bik', c['tb'], block_diag((arh[:, 0:CHUNK] + c['av']).astype(BF16)))
        yield
        ub = u.astype(BF16)
        y = arh[:, CHUNK:] + c['rkv'] + bmm('bij,bjk->bik', c['a_rb'], block_diag(ub))
        cross = bmm('bsi,bsj->bij', jnp.concatenate([ub, c['vv']], axis=1), c['bk'])
        yield
        upd = sum(jnp.where(head_of_lane == h, cross[:, h * RWKV_HEAD_DIM:(h + 1) * RWKV_HEAD_DIM], 0.0)
                  for h in range(CORE_GROUP))
        for b in range(bsz):
            for gi in range(ngroups):
                n = b * ngroups + gi
                lanes = slice(gi * gw, (gi + 1) * gw)
                ht_ref[n] = wc_ref[b, g, 0:1, lanes] * (ht[n] + upd[n])
                y_ref[b, g * CHUNK:(g + 1) * CHUNK, lanes] = y[n]

    def run_together(*gens):
        live = list(gens)
        while live:
            for gen in list(live):
                if next(gen, StopIteration) is StopIteration:
                    live.remove(gen)

    span = CORE_SPAN
    chunks = [dict() for _ in range(nchunks)]
    run_together(state_free_part(0, chunks))
    for g0 in range(span, nchunks, span):
        run_together(state_free_part(g0, chunks), recurrent_part(g0 - span, chunks))
    run_together(recurrent_part(nchunks - span, chunks))


def _rwkv_core(ar, bk, v, wc):
    bsz, nc = ar.shape[:2]
    g = CORE_CHUNKS
    gw = CORE_GROUP * RWKV_HEAD_DIM
    blk = lambda rows: pl.BlockSpec((bsz, g, rows, RWKV_W), lambda i: (0, i, 0, 0))
    y = pl.pallas_call(
        _rwkv_core_kernel,
        grid=(nc // g,),
        in_specs=[blk(2 * CHUNK), blk(2 * CHUNK), blk(CHUNK), blk(SUBLANES)],
        out_specs=pl.BlockSpec((bsz, g * CHUNK, RWKV_W), lambda i: (0, i, 0)),
        out_shape=jax.ShapeDtypeStruct((bsz, nc * CHUNK, RWKV_W), F32),
        scratch_shapes=[pltpu.VMEM((bsz * (RWKV_W // gw), RWKV_HEAD_DIM, gw), F32)],
        compiler_params=_params(),
    )(ar, bk, v, wc)
    return y.reshape(bsz * nc * CHUNK, RWKV_W)


def _mix_ffn_kernel(x_ref, c_ref, y_ref, gate_ref, bonus_ref, lng_ref, lnb_ref, ones_ref, w_ref,
                    g_ref, wup_ref, cw_ref, cb_ref, wdn_ref, o_ref, hist_ref, gs_ref, *, tiles_per_seq):
    i = pl.program_id(0)
    tt = x_ref.shape[0]

    @pl.when(i == 0)
    def _():
        hist_ref[...] = jnp.zeros_like(hist_ref)

    inv_n = 1.0 / RWKV_HEAD_DIM
    y = y_ref[...]
    half = ones_ref.shape[0]

    def head_sum(z):
        zb = z.astype(BF16)
        return jnp.concatenate([jnp.dot(zb[:, c0:c0 + half], ones_ref[...], preferred_element_type=F32)
                                for c0 in range(0, z.shape[1], half)], axis=1)

    yc = y - head_sum(y) * inv_n
    var = head_sum(yc * yc) * inv_n
    yn = yc * lax.rsqrt(var + RWKV_GN_EPS) * lng_ref[...] + lnb_ref[...]
    yo = ((yn + bonus_ref[...]) * gate_ref[...]).astype(BF16)
    mix = (jnp.dot(c_ref[...], w_ref[0:CONV_W, :], preferred_element_type=F32)
           + jnp.dot(yo, w_ref[CONV_W:, :], preferred_element_type=F32))
    x1 = x_ref[...] + mix
    hist = jnp.where(i % tiles_per_seq != 0, hist_ref[...], 0.0)
    o_ref[...] = _ffn_tile(x1, hist, g_ref, wup_ref, cw_ref, cb_ref, wdn_ref, gs_ref)
    hist_ref[...] = x1[tt - SUBLANES:, :]


def _mix_ffn(x2, c, y, gate, bonus, lng, lnb, ones, w_out, g, wup, cw, cb, wdn, layer, seq, tt):
    n, d = x2.shape
    row = lambda w: pl.BlockSpec((tt, w), lambda i: (i, 0))
    consts = [lng, lnb, ones, w_out]
    ffn_consts = [g, wup, cw, cb, wdn]
    kern = functools.partial(_mix_ffn_kernel, tiles_per_seq=seq // tt)
    return pl.pallas_call(
        kern,
        grid=(n // tt,),
        in_specs=[row(d), row(CONV_W), row(RWKV_W), row(RWKV_W), row(RWKV_W)]
                 + [_const_spec(c_.shape) for c_ in consts] + [_layer_spec(c_, layer) for c_ in ffn_consts],
        out_specs=row(d),
        out_shape=jax.ShapeDtypeStruct((n, d), F32),
        scratch_shapes=[pltpu.VMEM((SUBLANES, d), F32), pltpu.VMEM((tt + SUBLANES, D_FF), F32)],
        compiler_params=_params(),
    )(x2, c, y, gate, bonus, *consts, *ffn_consts)


def _ffn_tile(x, hist, g_ref, wup_ref, cw_ref, cb_ref, wdn_ref, gs_ref):
    tt = x.shape[0]
    xe = jnp.concatenate([hist, x], axis=0)
    h = _rms_norm(xe, g_ref[...]).astype(BF16)
    u = jnp.dot(h, wup_ref[...], preferred_element_type=F32)
    gs_ref[...] = u[:, :D_FF]
    gate = cb_ref[...]
    for j in range(FFN_CONV_K):
        gate = gate + cw_ref[j:j + 1, :] * gs_ref[pl.ds(SUBLANES - (FFN_CONV_K - 1) + j, tt), :]
    act = (gate * _sigmoid(gate) * u[SUBLANES:, D_FF:]).astype(BF16)
    return x + jnp.dot(act, wdn_ref[...], preferred_element_type=F32)


def _ffn_kernel(x_ref, halo_ref, g_ref, wup_ref, cw_ref, cb_ref, wdn_ref, o_ref, gs_ref, *, tiles_per_seq):
    keep = (pl.program_id(0) % tiles_per_seq != 0).astype(F32)
    o_ref[...] = _ffn_tile(x_ref[...], halo_ref[...] * keep, g_ref, wup_ref, cw_ref, cb_ref, wdn_ref, gs_ref)


def _layer_spec(stacked, layer):
    tail = stacked.shape[1:]
    return pl.BlockSpec((None,) + tail, lambda *_: (layer,) + (0,) * len(tail), pipeline_mode=pl.Buffered(1))


def _ffn(x2, g, wup, cw, cb, wdn, layer, seq, tt):
    n, d = x2.shape
    tps = seq // tt
    hb = tt // SUBLANES
    kern = functools.partial(_ffn_kernel, tiles_per_seq=tps)
    consts = [g, wup, cw, cb, wdn]
    return pl.pallas_call(
        kern,
        grid=(n // tt,),
        in_specs=[pl.BlockSpec((tt, d), lambda i: (i, 0)),
                  pl.BlockSpec((SUBLANES, d), lambda i: (jnp.maximum(i * hb - 1, 0), 0))]
                 + [_layer_spec(c, layer) for c in consts],
        out_specs=pl.BlockSpec((tt, d), lambda i: (i, 0)),
        out_shape=jax.ShapeDtypeStruct((n, d), F32),
        scratch_shapes=[pltpu.VMEM((tt + SUBLANES, D_FF), F32)],
        compiler_params=_params(),
    )(x2, x2, *consts)


def _rope_table_kernel(freq_ref, pos_ref, cs_ref):
    half = ROT_DIM // 2
    pos = pos_ref[...]
    for f in range(half):
        ang = pos * freq_ref[f]
        cs_ref[f] = jnp.cos(ang)
        cs_ref[half + f] = jnp.sin(ang)


def _rope_table(inv_freq, pos_dense):
    rows, lanes = pos_dense.shape
    return pl.pallas_call(
        _rope_table_kernel,
        in_specs=[pl.BlockSpec(memory_space=pltpu.SMEM), pl.BlockSpec((rows, lanes), lambda: (0, 0))],
        out_specs=pl.BlockSpec((ROT_DIM, rows, lanes), lambda: (0, 0, 0)),
        out_shape=jax.ShapeDtypeStruct((ROT_DIM, rows, lanes), F32),
    )(inv_freq, pos_dense)


def _qkv_kernel(x_ref, cs_ref, g_ref, w_ref, b_ref, qg_ref, kg_ref, expand_ref, ones_ref,
                q_ref, k_ref, v_ref):
    qd = N_HEADS * HEAD_DIM
    kd = N_KV_HEADS * HEAD_DIM
    h = _rms_norm(x_ref[...], g_ref[...]).astype(BF16)

    def project(c0, width):
        return jnp.dot(h, w_ref[:, c0:c0 + width], preferred_element_type=F32) + b_ref[:, c0:c0 + width]

    lanes = expand_ref.shape[1] // 3
    cs_hi, cs_lo, _ = _split3(cs_ref[...])
    tab = (jnp.dot(cs_hi, expand_ref[...], preferred_element_type=F32)
           + jnp.dot(cs_lo, expand_ref[...], preferred_element_type=F32))
    dim = lax.broadcasted_iota(jnp.int32, (1, lanes), 1) % HEAD_DIM
    cos = tab[:, :lanes] + (dim >= ROT_DIM).astype(F32)
    sin_a = tab[:, lanes:2 * lanes]
    sin_b = tab[:, 2 * lanes:]
    half = ROT_DIM // 2

    def norm_rope(c0, gain, scale, dst_ref, head0):
        z = project(c0, blk)
        yield
        ss = jnp.dot((z * z).astype(BF16), ones_ref[...], preferred_element_type=F32)
        yield
        z = z * lax.rsqrt(ss * (1.0 / HEAD_DIM) + NORM_EPS) * gain
        for j, c0 in enumerate(range(0, z.shape[1], lanes)):
            zz = z[:, c0:c0 + lanes]
            rot = (zz * cos + pltpu.roll(zz, lanes - half, 1) * sin_a + pltpu.roll(zz, half, 1) * sin_b) * scale
            dst_ref[0, head0 + 2 * j] = rot[:, :HEAD_DIM].astype(BF16)
            dst_ref[0, head0 + 2 * j + 1] = rot[:, HEAD_DIM:].astype(BF16)
            yield

    def values():
        vt = project(qd + kd, kd).T
        yield
        for j in range(N_KV_HEADS):
            v_ref[0, j] = vt[j * HEAD_DIM:(j + 1) * HEAD_DIM, :].astype(BF16)
        yield

    blk = 4 * HEAD_DIM
    live = [norm_rope(c0, qg_ref[...], HEAD_DIM ** -0.5 * LOG2_E, q_ref, c0 // HEAD_DIM) for c0 in range(0, qd, blk)]
    live += [norm_rope(qd, kg_ref[...], 1.0, k_ref, 0), values()]
    while live:
        live = [u for u in live if next(u, StopIteration) is not StopIteration]


def _qkv(x2, cs, g, w, b, qg, kg, expand, ones, bsz, seq, tt):
    n, d = x2.shape
    tps = seq // tt
    consts = [g, w, b, qg, kg, expand, ones]
    hm = lambda nh: pl.BlockSpec((1, nh, tt, HEAD_DIM), lambda i: (i // tps, 0, i % tps, 0))
    return pl.pallas_call(
        _qkv_kernel,
        grid=(n // tt,),
        in_specs=[pl.BlockSpec((tt, d), lambda i: (i, 0)), pl.BlockSpec((tt, ROT_DIM), lambda i: (i, 0))]
                 + [_const_spec(c.shape) for c in consts],
        out_specs=[hm(N_HEADS), hm(N_KV_HEADS),
                   pl.BlockSpec((1, N_KV_HEADS, HEAD_DIM, tt), lambda i: (i // tps, 0, 0, i % tps))],
        out_shape=[jax.ShapeDtypeStruct((bsz, N_HEADS, seq, HEAD_DIM), BF16),
                   jax.ShapeDtypeStruct((bsz, N_KV_HEADS, seq, HEAD_DIM), BF16),
                   jax.ShapeDtypeStruct((bsz, N_KV_HEADS, HEAD_DIM, seq), BF16)],
        compiler_params=_params(),
    )(x2, cs, *consts)


def _attn_kernel(sink_ref, q_ref, kc_ref, kp_ref, vc_ref, vp_ref, x_ref, w_ref, b_ref, out_ref, *, q_blocks):
    seq_start = pl.program_id(1) == 0
    blk = WINDOW
    cols = GROUP * blk
    c = lax.broadcasted_iota(jnp.int32, (2 * blk, cols), 0)
    t = lax.broadcasted_iota(jnp.int32, (2 * blk, cols), 1) % blk
    d = c - t
    band = (d >= 1) & (d <= blk)
    head_of_lane = lax.broadcasted_iota(jnp.int32, (1, cols), 1) // blk
    def unit(qb, g, outs):
        valid = band & ((c >= blk) | jnp.logical_not(seq_start)) if qb == 0 else band
        q = q_ref[0, g * GROUP:(g + 1) * GROUP, qb * blk:(qb + 1) * blk, :].reshape(cols, HEAD_DIM)
        if qb == 0:
            kprev, vprev = kp_ref[0, g], vp_ref[0, g]
        else:
            kprev = kc_ref[0, g, (qb - 1) * blk:qb * blk, :]
            vprev = vc_ref[0, g, :, (qb - 1) * blk:qb * blk]
        k = jnp.concatenate([kprev, kc_ref[0, g, qb * blk:(qb + 1) * blk, :]], axis=0)
        vt = jnp.concatenate([vprev, vc_ref[0, g, :, qb * blk:(qb + 1) * blk]], axis=1)
        s = lax.dot_general(k, q, NT_DIMS, preferred_element_type=F32)
        yield
        s = jnp.where(valid, s, -jnp.inf)
        sink = jnp.zeros((1, cols), F32)
        for j in range(GROUP):
            sink = jnp.where(head_of_lane == j, sink_ref[g * GROUP + j] * LOG2_E, sink)
        m = jnp.maximum(jnp.max(s, axis=0, keepdims=True), sink)
        p = jnp.exp2(s - m)
        den = jnp.sum(p, axis=0, keepdims=True) + jnp.exp2(sink - m)
        yield
        ot = jnp.dot(vt, p.astype(BF16), preferred_element_type=F32) / den
        yield
        o = ot.T
        outs[g] = [o[j * blk:(j + 1) * blk] for j in range(GROUP)]
        yield

    blocks = []
    for qb0 in range(0, q_blocks, ATTN_TOGETHER):
        qbs = range(qb0, min(qb0 + ATTN_TOGETHER, q_blocks))
        outs = {qb: [None] * N_KV_HEADS for qb in qbs}
        live = [unit(qb, g, outs[qb]) for qb in qbs for g in range(N_KV_HEADS)]
        while live:
            live = [u for u in live if next(u, StopIteration) is not StopIteration]
        for qb in qbs:
            blocks.append(jnp.concatenate([o for group in outs[qb] for o in group], axis=1).astype(BF16))
    o_all = jnp.concatenate(blocks, axis=0)
    out_ref[...] = x_ref[...] + jnp.dot(o_all, w_ref[...], preferred_element_type=F32) + b_ref[...]


def _attention(sinks, q, k, v, x2, w_o, b_o, q_blocks):
    bsz, _, seq, _ = q.shape
    d = x2.shape[1]
    tq = q_blocks * WINDOW
    steps = seq // tq
    kern = functools.partial(_attn_kernel, q_blocks=q_blocks)
    cur = lambda nh: pl.BlockSpec((1, nh, tq, HEAD_DIM), lambda b, i: (b, 0, i, 0))
    prev = pl.BlockSpec((1, N_KV_HEADS, WINDOW, HEAD_DIM), lambda b, i: (b, 0, jnp.maximum(i * q_blocks - 1, 0), 0))
    rows = pl.BlockSpec((tq, d), lambda b, i: (b * steps + i, 0))
    vcur = pl.BlockSpec((1, N_KV_HEADS, HEAD_DIM, tq), lambda b, i: (b, 0, 0, i))
    vprev = pl.BlockSpec((1, N_KV_HEADS, HEAD_DIM, WINDOW), lambda b, i: (b, 0, 0, jnp.maximum(i * q_blocks - 1, 0)))
    return pl.pallas_call(
        kern,
        grid=(bsz, steps),
        in_specs=[pl.BlockSpec(memory_space=pltpu.SMEM), cur(N_HEADS), cur(N_KV_HEADS), prev, vcur, vprev,
                  rows, _const_spec(w_o.shape), _const_spec(b_o.shape)],
        out_specs=rows,
        out_shape=jax.ShapeDtypeStruct((bsz * seq, d), F32),
        compiler_params=_params(2),
    )(sinks, q, k, k, v, v, x2, w_o, b_o)


def _block_ones(width, group):
    idx = np.arange(width) // group
    return jnp.asarray(idx[:, None] == idx[None, :], BF16)


def _chunk_tril(rows):
    idx = np.arange(rows)
    same = (idx[:, None] // CHUNK) == (idx[None, :] // CHUNK)
    return jnp.asarray(same & (idx[None, :] <= idx[:, None]), BF16)


def _rope_expand(lanes=128):
    half = ROT_DIM // 2
    dim = np.arange(lanes) % HEAD_DIM
    f = np.arange(half)[:, None]
    hit = (dim[None, :] % half == f)
    zero = np.zeros((half, lanes))
    cos_rows = np.concatenate([hit & (dim < ROT_DIM)[None, :], zero, zero], axis=1)
    sin_rows = np.concatenate([zero, -1.0 * (hit & (dim < half)[None, :]),
                               hit & ((dim >= half) & (dim < ROT_DIM))[None, :]], axis=1)
    return jnp.asarray(np.concatenate([cos_rows, sin_rows], axis=0), BF16)


def _row(v):
    return v.reshape(1, -1)


def kernel(x, positions, ab_norm_g, ab_w_in, conv_in_b, conv_dw_w, conv_dw_b, conv_ln_g, conv_ln_b, rwkv_mu, rwkv_w0, rwkv_w2, rwkv_a0, rwkv_a2, rwkv_g2, rwkv_k_k, rwkv_k_a, rwkv_r_k, rwkv_ln_g, rwkv_ln_b, ab_w_out, attn_norm_g, attn_w_qkv, attn_b_qkv, attn_q_norm_g, attn_k_norm_g, attn_sinks, attn_w_o, attn_b_o, ffn_norm_g, ffn_w_up, ffn_conv_w, ffn_conv_b, ffn_w_down):
    bsz, seq, d = x.shape
    depth = ffn_norm_g.shape[0]
    n = bsz * seq
    tt = 512
    ffn_tt = 512
    qkv_tt = 1024
    x2 = x.reshape(n, d)
    ones_rwkv = _block_ones(RWKV_W, RWKV_HEAD_DIM)
    ones_quad = _block_ones(4 * HEAD_DIM, HEAD_DIM)
    tri = _chunk_tril(256)
    rope_expand = _rope_expand()
    ffn_g = ffn_norm_g[:, None, :]
    ffn_cb = ffn_conv_b[:, None, :]
    ffn_up = ffn_w_up.astype(BF16)
    ffn_down = ffn_w_down.astype(BF16)

    for layer in range(depth):
        i = layer // 2
        if layer % 2 == 0:
            zeros = jnp.zeros((LORA_DECAY, RWKV_W), F32)
            w2a2 = jnp.concatenate([jnp.concatenate([rwkv_w2[i], zeros], axis=1),
                                    jnp.concatenate([zeros, rwkv_a2[i]], axis=1)], axis=0).astype(BF16)
            c, ar, bk, vv, wc, gate, bonus = _front(
                x2, _row(ab_norm_g[i]), ab_w_in[i].astype(BF16),
                _row(conv_in_b[i]), conv_dw_w[i], _row(conv_dw_b[i]), _row(conv_ln_g[i]), _row(conv_ln_b[i]),
                _row(rwkv_mu[i]), _row(rwkv_w0[i]), w2a2, _row(rwkv_a0[i]), rwkv_g2[i].astype(BF16),
                _row(rwkv_k_k[i]), _row(rwkv_k_a[i]), _row(rwkv_r_k[i]), ones_rwkv, tri, bsz, seq, tt)
            y = _rwkv_core(ar, bk, vv, wc)
            x2 = _mix_ffn(x2, c, y, gate, bonus, _row(rwkv_ln_g[i]), _row(rwkv_ln_b[i]), ones_quad,
                          ab_w_out[i].astype(BF16), ffn_g, ffn_up, ffn_conv_w, ffn_cb, ffn_down, layer, seq, ffn_tt)
            continue
        else:
            half = ROT_DIM // 2
            inv_freq = ROPE_THETA ** (-(jnp.arange(half, dtype=F32) * 2.0) / ROT_DIM)
            lanes = 128
            cs = _rope_table(inv_freq, positions.astype(F32).reshape(n // lanes, lanes))
            cs = cs.reshape(ROT_DIM, n).T
            q, k, v = _qkv(x2, cs, _row(attn_norm_g[i]), attn_w_qkv[i].astype(BF16), _row(attn_b_qkv[i]),
                           _row(jnp.tile(attn_q_norm_g[i], 4)), _row(jnp.tile(attn_k_norm_g[i], 4)),
                           rope_expand, ones_quad, bsz, seq, qkv_tt)
            x2 = _attention(attn_sinks[i], q, k, v, x2, attn_w_o[i].astype(BF16), _row(attn_b_o[i]), 8)
        x2 = _ffn(x2, ffn_g, ffn_up, ffn_conv_w, ffn_cb, ffn_down, layer, seq, ffn_tt)
    return x2.reshape(bsz, seq, d)
```

```python
import functools

import jax
import jax.numpy as jnp
import numpy as np
from jax import lax
from jax.experimental import pallas as pl
from jax.experimental.pallas import tpu as pltpu

F32 = jnp.float32
BF16 = jnp.bfloat16

CONV_W = 512
CONV_K = 31
CONV_LN_EPS = 1e-5
RWKV_HEADS = 8
RWKV_HEAD_DIM = 64
RWKV_W = RWKV_HEADS * RWKV_HEAD_DIM
LORA_DECAY = 64
LORA_ICLR = 64
LORA_GATE = 128
RWKV_GN_EPS = RWKV_HEAD_DIM * 1e-5
RWKV_IN = 3 * RWKV_W + LORA_DECAY + LORA_ICLR + LORA_GATE
HEAD_DIM = 64
N_HEADS = 16
N_KV_HEADS = 4
GROUP = N_HEADS // N_KV_HEADS
WINDOW = 128
ROT_DIM = 16
ROPE_THETA = 500000.0
D_FF = 2816
FFN_CONV_K = 3
NORM_EPS = 1e-6
LOG2_E = 1.4426950408889634

CHUNK = 64
CORE_CHUNKS = 16
CORE_SPAN = 2
CORE_GROUP = 4
SUBLANES = 8
CONV_HALO = 32
ATTN_TOGETHER = 4
V7X_VMEM_LIMIT = 56 * 1024 * 1024

NT_DIMS = (((1,), (1,)), ((), ()))


def _params(n_axes=1):
    return pltpu.CompilerParams(dimension_semantics=("arbitrary",) * n_axes,
                                vmem_limit_bytes=V7X_VMEM_LIMIT)


def _const_spec(shape):
    nd = len(shape)
    return pl.BlockSpec(shape, lambda *_: (0,) * nd, pipeline_mode=pl.Buffered(1))


def _rms_norm(x, g):
    return x * lax.rsqrt(jnp.mean(x * x, axis=-1, keepdims=True) + NORM_EPS) * g


def _sigmoid(z):
    return 1.0 / (1.0 + jnp.exp(-z))


def _split3(z):
    hi = z.astype(BF16)
    r1 = z - hi.astype(F32)
    mid = r1.astype(BF16)
    lo = (r1 - mid.astype(F32)).astype(BF16)
    return hi, mid, lo


def _conv_shift_copies(ext_ref, tt):
    span = tt + CONV_HALO - SUBLANES
    for s in range(1, SUBLANES):
        ext_ref[s, 0:span, :] = ext_ref[0, pl.ds(s, span), :]


def _conv_taps(ext_ref, dww_ref, dwb_ref, lng_ref, lnb_ref, c_ref, tt, row_block):
    first = CONV_HALO - (CONV_K - 1)
    for r0 in range(0, tt, row_block):
        acc = jnp.broadcast_to(dwb_ref[...], (row_block, CONV_W))
        for s in range(SUBLANES):
            taps = [j for j in range(CONV_K) if (first + j) % SUBLANES == s]
            lo = first + taps[0] - s
            hi = first + taps[-1] - s
            win = ext_ref[s, pl.ds(r0 + lo, row_block + hi - lo), :]
            for j in taps:
                a = first + j - s - lo
                acc = acc + dww_ref[j:j + 1, :] * win[a:a + row_block]
        mu = jnp.mean(acc, axis=-1, keepdims=True)
        xc = acc - mu
        var = jnp.mean(xc * xc, axis=-1, keepdims=True)
        z = xc * lax.rsqrt(var + CONV_LN_EPS) * lng_ref[...] + lnb_ref[...]
        c_ref[r0:r0 + row_block, :] = (z * _sigmoid(z)).astype(c_ref.dtype)


def _convmod(pc, inb, dww, dwb, lng, lnb, seq, tt):
    n = pc.shape[0]
    tps = seq // tt
    hb = tt // CONV_HALO
    kern = functools.partial(_convmod_kernel, tiles_per_seq=tps, row_block=64)
    return pl.pallas_call(
        kern,
        grid=(n // tt,),
        in_specs=[pl.BlockSpec((tt, 2 * CONV_W), lambda i: (i, 0)),
                  pl.BlockSpec((CONV_HALO, 2 * CONV_W), lambda i: (jnp.maximum(i * hb - 1, 0), 0)),
                  _const_spec(inb.shape), _const_spec(dww.shape), _const_spec(dwb.shape),
                  _const_spec(lng.shape), _const_spec(lnb.shape)],
        out_specs=pl.BlockSpec((tt, CONV_W), lambda i: (i, 0)),
        out_shape=jax.ShapeDtypeStruct((n, CONV_W), BF16),
        scratch_shapes=[pltpu.VMEM((SUBLANES, tt + CONV_HALO, CONV_W), F32)],
        compiler_params=_params(),
    )(pc, pc, inb, dww, dwb, lng, lnb)


def _rwkv_prep_body(ext_ref, mu_ref, w0_ref, w2a2_ref, a0_ref, g2_ref, kk_ref, ka_ref, rk_ref, ones_ref, tri_ref,
                    ar_ref, bk_ref, v_ref, wc_ref, gate_ref, bonus_ref, tt):
    rw = ext_ref[SUBLANES:, :]
    rw = rw + (ext_ref[pl.ds(SUBLANES - 1, tt), :] - rw) * mu_ref[...]

    r = rw[:, 0:RWKV_W]
    k = rw[:, RWKV_W:2 * RWKV_W]
    v = rw[:, 2 * RWKV_W:3 * RWKV_W]
    wa = rw[:, 3 * RWKV_W:3 * RWKV_W + LORA_DECAY + LORA_ICLR]
    gd = rw[:, 3 * RWKV_W + LORA_DECAY + LORA_ICLR:]

    lane = lax.broadcasted_iota(jnp.int32, wa.shape, 1)
    z = jnp.where(lane < LORA_DECAY, jnp.tanh(wa), wa)
    proj = jnp.dot(z.astype(BF16), w2a2_ref[...], preferred_element_type=F32)
    zw = -(w0_ref[...] + proj[:, :RWKV_W])
    softplus = jnp.maximum(zw, 0.0) + jnp.log(1.0 + jnp.exp(-jnp.abs(zw)))
    lw = -jnp.exp(-softplus - 0.5)
    a = _sigmoid(a0_ref[...] + proj[:, RWKV_W:])
    gate_ref[...] = jnp.dot(_sigmoid(gd).astype(BF16), g2_ref[...], preferred_element_type=F32)

    kk = k * kk_ref[...]
    ss = jnp.dot((kk * kk).astype(BF16), ones_ref[...], preferred_element_type=F32)
    kk = kk / jnp.maximum(jnp.sqrt(ss), 1e-12)
    k = k * (1.0 + (a - 1.0) * ka_ref[...])
    rk = jnp.dot((r * k * rk_ref[...]).astype(BF16), ones_ref[...], preferred_element_type=F32)
    bonus_ref[...] = rk * v

    half = tri_ref.shape[0]
    cums = []
    for r0 in range(0, tt, half):
        parts = _split3(lw[r0:r0 + half])
        cums.append(sum(jnp.dot(tri_ref[...], p, preferred_element_type=F32) for p in parts))
    cum = jnp.concatenate(cums, axis=0)
    e_cum = jnp.exp(cum)
    e_inv = jnp.exp(-cum)
    rt = r * e_cum
    at = -kk * jnp.exp(cum - lw)
    kt = k * e_inv
    bt = kk * a * e_inv

    nch = tt // CHUNK
    for c in range(nch):
        rows = slice(c * CHUNK, (c + 1) * CHUNK)
        ar_ref[0, c, 0:CHUNK, :] = at[rows].astype(BF16)
        ar_ref[0, c, CHUNK:, :] = rt[rows].astype(BF16)
        bk_ref[0, c, 0:CHUNK, :] = bt[rows].astype(BF16)
        bk_ref[0, c, CHUNK:, :] = kt[rows].astype(BF16)
        v_ref[0, c, :, :] = v[rows].astype(BF16)
        wc_ref[0, c, :, :] = jnp.broadcast_to(e_cum[(c + 1) * CHUNK - 1:(c + 1) * CHUNK], (SUBLANES, RWKV_W))


def _front_kernel(x0_ref, xn_ref, ng_ref, win_ref, inb_ref, dww_ref, dwb_ref, lng_ref, lnb_ref,
                  mu_ref, w0_ref, w2a2_ref, a0_ref, g2_ref, kk_ref, ka_ref, rk_ref, ones_ref, tri_ref,
                  c_ref, ar_ref, bk_ref, v_ref, wc_ref, gate_ref, bonus_ref, ext_ref, prc_ref,
                  *, tiles_per_seq, row_block):
    i = pl.program_id(0)
    tt = xn_ref.shape[0]

    def project(x_ref):
        h = _rms_norm(x_ref[...], ng_ref[...])
        return jnp.dot(h.astype(BF16), win_ref[...], preferred_element_type=F32)

    def make_current(p, conv_hist, shift_hist):
        z = p[:, :2 * CONV_W] + inb_ref[...]
        ext_ref[0, 0:CONV_HALO, :] = conv_hist
        ext_ref[0, CONV_HALO:, :] = z[:, :CONV_W] * _sigmoid(z[:, CONV_W:])
        _conv_shift_copies(ext_ref, tt)
        prc_ref[0:SUBLANES, :] = shift_hist
        prc_ref[SUBLANES:, :] = p[:, 2 * CONV_W:]

    @pl.when(i == 0)
    def _():
        make_current(project(x0_ref), jnp.zeros((CONV_HALO, CONV_W), F32), jnp.zeros((SUBLANES, RWKV_IN), F32))

    p_next = project(xn_ref)
    _conv_taps(ext_ref, dww_ref, dwb_ref, lng_ref, lnb_ref, c_ref, tt, row_block)
    _rwkv_prep_body(prc_ref, mu_ref, w0_ref, w2a2_ref, a0_ref, g2_ref, kk_ref, ka_ref, rk_ref, ones_ref, tri_ref,
                    ar_ref, bk_ref, v_ref, wc_ref, gate_ref, bonus_ref, tt)
    same_seq = (i + 1) % tiles_per_seq != 0
    make_current(p_next,
                 jnp.where(same_seq, ext_ref[0, tt:tt + CONV_HALO, :], 0.0),
                 jnp.where(same_seq, prc_ref[tt:tt + SUBLANES, :], 0.0))


def _front(x2, ng, w_in, inb, dww, dwb, lng, lnb, mu, w0, w2a2, a0, g2, k_k, k_a, r_k, ones, tri, bsz, seq, tt):
    n, d = x2.shape
    tps = seq // tt
    steps = n // tt
    nch = tt // CHUNK
    nc = seq // CHUNK
    kern = functools.partial(_front_kernel, tiles_per_seq=tps, row_block=64)
    hm = lambda rows: pl.BlockSpec((1, nch, rows, RWKV_W), lambda i: (i // tps, i % tps, 0, 0))
    hm_shape = lambda rows, dt: jax.ShapeDtypeStruct((bsz, nc, rows, RWKV_W), dt)
    consts = [ng, w_in, inb, dww, dwb, lng, lnb, mu, w0, w2a2, a0, g2, k_k, k_a, r_k, ones, tri]
    row = lambda w: pl.BlockSpec((tt, w), lambda i: (i, 0))
    return pl.pallas_call(
        kern,
        grid=(steps,),
        in_specs=[pl.BlockSpec((tt, d), lambda i: (0, 0)),
                  pl.BlockSpec((tt, d), lambda i: (jnp.minimum(i + 1, steps - 1), 0))]
                 + [_const_spec(c.shape) for c in consts],
        out_specs=[row(CONV_W), hm(2 * CHUNK), hm(2 * CHUNK), hm(CHUNK), hm(SUBLANES), row(RWKV_W), row(RWKV_W)],
        out_shape=[jax.ShapeDtypeStruct((n, CONV_W), BF16),
                   hm_shape(2 * CHUNK, BF16), hm_shape(2 * CHUNK, BF16), hm_shape(CHUNK, BF16), hm_shape(SUBLANES, F32),
                   jax.ShapeDtypeStruct((n, RWKV_W), F32), jax.ShapeDtypeStruct((n, RWKV_W), F32)],
        scratch_shapes=[pltpu.VMEM((SUBLANES, tt + CONV_HALO, CONV_W), F32),
                        pltpu.VMEM((tt + SUBLANES, RWKV_IN), F32)],
        compiler_params=_params(),
    )(x2, x2, *consts)


def _rwkv_core_kernel(ar_ref, bk_ref, v_ref, wc_ref, y_ref, ht_ref):
    @pl.when(pl.program_id(0) == 0)
    def _():
        ht_ref[...] = jnp.zeros_like(ht_ref)

    bsz, nchunks = ar_ref.shape[0], ar_ref.shape[1]
    gw = CORE_GROUP * RWKV_HEAD_DIM
    ngroups = RWKV_W // gw
    npar = bsz * ngroups
    row = lax.broadcasted_iota(jnp.int32, (CHUNK, gw), 0)
    col = lax.broadcasted_iota(jnp.int32, (CHUNK, gw), 1) % RWKV_HEAD_DIM
    strict = col < row
    incl = col <= row
    eye = (col == row).astype(F32)
    blk_r = lax.broadcasted_iota(jnp.int32, (gw, gw), 0) // RWKV_HEAD_DIM
    blk_c = lax.broadcasted_iota(jnp.int32, (gw, gw), 1) // RWKV_HEAD_DIM
    same_head = blk_r == blk_c
    head_of_lane = lax.broadcasted_iota(jnp.int32, (CHUNK, gw), 1) // RWKV_HEAD_DIM
    bmm = functools.partial(jnp.einsum, preferred_element_type=F32)

    def block_diag(z):
        tiled = jnp.concatenate([z] * CORE_GROUP, axis=1)
        return jnp.where(same_head[None], tiled, jnp.zeros_like(tiled))

    def grouped(ref, g0):
        return jnp.stack([ref[b, g, :, gi * gw:(gi + 1) * gw]
                          for g in range(g0, g0 + span) for b in range(bsz) for gi in range(ngroups)], axis=0)

    def state_free_part(g0, outs):
        ar = grouped(ar_ref, g0)
        bk = grouped(bk_ref, g0)
        vv = grouped(v_ref, g0)
        rhs = jnp.concatenate([block_diag(bk[:, 0:CHUNK]), block_diag(bk[:, CHUNK:])], axis=1)
        s = bmm('bmk,bnk->bmn', ar, rhs)
        yield
        a_ab = jnp.where(strict[None], s[:, 0:CHUNK, 0:gw], 0.0)
        a_ak = jnp.where(strict[None], s[:, 0:CHUNK, gw:], 0.0).astype(BF16)
        a_rb = jnp.where(incl[None], s[:, CHUNK:, 0:gw], 0.0).astype(BF16)
        a_rk = jnp.where(incl[None], s[:, CHUNK:, gw:], 0.0).astype(BF16)
        p = eye[None] + a_ab
        qb = a_ab.astype(BF16)
        q = bmm('bij,bjk->bik', qb, block_diag(qb))
        yield
        for _ in range(4):
            qb = q.astype(BF16)
            both = bmm('bij,bjk->bik', jnp.concatenate([qb, p.astype(BF16)], axis=1), block_diag(qb))
            yield
            q = both[:, 0:CHUNK]
            p = p + both[:, CHUNK:]
        p = p + bmm('bij,bjk->bik', p.astype(BF16), block_diag(q.astype(BF16)))
        yield
        both = bmm('bij,bjk->bik', jnp.concatenate([a_ak, a_rk], axis=1), block_diag(vv))
        tb = p.astype(BF16)
        for j in range(span):
            sl = slice(j * npar, (j + 1) * npar)
            outs[g0 + j].update(ar=ar[sl], bk=bk[sl], vv=vv[sl], a_rb=a_rb[sl], tb=tb[sl],
                                av=both[sl, 0:CHUNK], rkv=both[sl, CHUNK:])
        yield

    def recurrent_part(g0, outs):
        for g in range(g0, g0 + span):
            yield from recurrent_chunk(g, outs[g])

    def recurrent_chunk(g, c):
        ht = ht_ref[...]
        arh = bmm('bmk,bnk->bmn', c['ar'], block_diag(ht.astype(BF16)))
        yield
        u = bmm('bij,bjk->bik', c['tb'], block_diag((arh[:, 0:CHUNK] + c['av']).astype(BF16)))
        yield
        ub = u.astype(BF16)
        y = arh[:, CHUNK:] + c['rkv'] + bmm('bij,bjk->bik', c['a_rb'], block_diag(ub))
        cross = bmm('bsi,bsj->bij', jnp.concatenate([ub, c['vv']], axis=1), c['bk'])
        yield
        upd = sum(jnp.where(head_of_lane == h, cross[:, h * RWKV_HEAD_DIM:(h + 1) * RWKV_HEAD_DIM], 0.0)
                  for h in range(CORE_GROUP))
        for b in range(bsz):
            for gi in range(ngroups):
                n = b * ngroups + gi
                lanes = slice(gi * gw, (gi + 1) * gw)
                ht_ref[n] = wc_ref[b, g, 0:1, lanes] * (ht[n] + upd[n])
                y_ref[b, g * CHUNK:(g + 1) * CHUNK, lanes] = y[n]

    def run_together(*gens):
        live = list(gens)
        while live:
            for gen in list(live):
                if next(gen, StopIteration) is StopIteration:
                    live.remove(gen)

    span = CORE_SPAN
    chunks = [dict() for _ in range(nchunks)]
    run_together(state_free_part(0, chunks))
    for g0 in range(span, nchunks, span):
        run_together(state_free_part(g0, chunks), recurrent_part(g0 - span, chunks))
    run_together(recurrent_part(nchunks - span, chunks))


def _rwkv_core(ar, bk, v, wc):
    bsz, nc = ar.shape[:2]
    g = CORE_CHUNKS
    gw = CORE_GROUP * RWKV_HEAD_DIM
    blk = lambda rows: pl.BlockSpec((bsz, g, rows, RWKV_W), lambda i: (0, i, 0, 0))
    y = pl.pallas_call(
        _rwkv_core_kernel,
        grid=(nc // g,),
        in_specs=[blk(2 * CHUNK), blk(2 * CHUNK), blk(CHUNK), blk(SUBLANES)],
        out_specs=pl.BlockSpec((bsz, g * CHUNK, RWKV_W), lambda i: (0, i, 0)),
        out_shape=jax.ShapeDtypeStruct((bsz, nc * CHUNK, RWKV_W), F32),
        scratch_shapes=[pltpu.VMEM((bsz * (RWKV_W // gw), RWKV_HEAD_DIM, gw), F32)],
        compiler_params=_params(),
    )(ar, bk, v, wc)
    return y.reshape(bsz * nc * CHUNK, RWKV_W)


def _mix_ffn_kernel(x_ref, c_ref, y_ref, gate_ref, bonus_ref, lng_ref, lnb_ref, ones_ref, w_ref,
                    g_ref, wup_ref, cw_ref, cb_ref, wdn_ref, o_ref, hist_ref, gs_ref, *, tiles_per_seq):
    i = pl.program_id(0)
    tt = x_ref.shape[0]

    @pl.when(i == 0)
    def _():
        hist_ref[...] = jnp.zeros_like(hist_ref)

    inv_n = 1.0 / RWKV_HEAD_DIM
    y = y_ref[...]
    half = ones_ref.shape[0]

    def head_sum(z):
        zb = z.astype(BF16)
        return jnp.concatenate([jnp.dot(zb[:, c0:c0 + half], ones_ref[...], preferred_element_type=F32)
                                for c0 in range(0, z.shape[1], half)], axis=1)

    yc = y - head_sum(y) * inv_n
    var = head_sum(yc * yc) * inv_n
    yn = yc * lax.rsqrt(var + RWKV_GN_EPS) * lng_ref[...] + lnb_ref[...]
    yo = ((yn + bonus_ref[...]) * gate_ref[...]).astype(BF16)
    mix = (jnp.dot(c_ref[...], w_ref[0:CONV_W, :], preferred_element_type=F32)
           + jnp.dot(yo, w_ref[CONV_W:, :], preferred_element_type=F32))
    x1 = x_ref[...] + mix
    hist = jnp.where(i % tiles_per_seq != 0, hist_ref[...], 0.0)
    o_ref[...] = _ffn_tile(x1, hist, g_ref, wup_ref, cw_ref, cb_ref, wdn_ref, gs_ref)
    hist_ref[...] = x1[tt - SUBLANES:, :]


def _mix_ffn(x2, c, y, gate, bonus, lng, lnb, ones, w_out, g, wup, cw, cb, wdn, layer, seq, tt):
    n, d = x2.shape
    row = lambda w: pl.BlockSpec((tt, w), lambda i: (i, 0))
    consts = [lng, lnb, ones, w_out]
    ffn_consts = [g, wup, cw, cb, wdn]
    kern = functools.partial(_mix_ffn_kernel, tiles_per_seq=seq // tt)
    return pl.pallas_call(
        kern,
        grid=(n // tt,),
        in_specs=[row(d), row(CONV_W), row(RWKV_W), row(RWKV_W), row(RWKV_W)]
                 + [_const_spec(c_.shape) for c_ in consts] + [_layer_spec(c_, layer) for c_ in ffn_consts],
        out_specs=row(d),
        out_shape=jax.ShapeDtypeStruct((n, d), F32),
        scratch_shapes=[pltpu.VMEM((SUBLANES, d), F32), pltpu.VMEM((tt + SUBLANES, D_FF), F32)],
        compiler_params=_params(),
    )(x2, c, y, gate, bonus, *consts, *ffn_consts)


def _ffn_tile(x, hist, g_ref, wup_ref, cw_ref, cb_ref, wdn_ref, gs_ref):
    tt = x.shape[0]
    xe = jnp.concatenate([hist, x], axis=0)
    h = _rms_norm(xe, g_ref[...]).astype(BF16)
    u = jnp.dot(h, wup_ref[...], preferred_element_type=F32)
    gs_ref[...] = u[:, :D_FF]
    gate = cb_ref[...]
    for j in range(FFN_CONV_K):
        gate = gate + cw_ref[j:j + 1, :] * gs_ref[pl.ds(SUBLANES - (FFN_CONV_K - 1) + j, tt), :]
    act = (gate * _sigmoid(gate) * u[SUBLANES:, D_FF:]).astype(BF16)
    return x + jnp.dot(act, wdn_ref[...], preferred_element_type=F32)


def _ffn_kernel(x_ref, halo_ref, g_ref, wup_ref, cw_ref, cb_ref, wdn_ref, o_ref, gs_ref, *, tiles_per_seq):
    keep = (pl.program_id(0) % tiles_per_seq != 0).astype(F32)
    o_ref[...] = _ffn_tile(x_ref[...], halo_ref[...] * keep, g_ref, wup_ref, cw_ref, cb_ref, wdn_ref, gs_ref)


def _layer_spec(stacked, layer):
    tail = stacked.shape[1:]
    return pl.BlockSpec((None,) + tail, lambda *_: (layer,) + (0,) * len(tail), pipeline_mode=pl.Buffered(1))


def _ffn(x2, g, wup, cw, cb, wdn, layer, seq, tt):
    n, d = x2.shape
    tps = seq // tt
    hb = tt // SUBLANES
    kern = functools.partial(_ffn_kernel, tiles_per_seq=tps)
    consts = [g, wup, cw, cb, wdn]
    return pl.pallas_call(
        kern,
        grid=(n // tt,),
        in_specs=[pl.BlockSpec((tt, d), lambda i: (i, 0)),
                  pl.BlockSpec((SUBLANES, d), lambda i: (jnp.maximum(i * hb - 1, 0), 0))]
                 + [_layer_spec(c, layer) for c in consts],
        out_specs=pl.BlockSpec((tt, d), lambda i: (i, 0)),
        out_shape=jax.ShapeDtypeStruct((n, d), F32),
        scratch_shapes=[pltpu.VMEM((tt + SUBLANES, D_FF), F32)],
        compiler_params=_params(),
    )(x2, x2, *consts)


def _rope_table_kernel(freq_ref, pos_ref, cs_ref):
    half = ROT_DIM // 2
    pos = pos_ref[...]
    for f in range(half):
        ang = pos * freq_ref[f]
        cs_ref[f] = jnp.cos(ang)
        cs_ref[half + f] = jnp.sin(ang)


def _rope_table(inv_freq, pos_dense):
    rows, lanes = pos_dense.shape
    return pl.pallas_call(
        _rope_table_kernel,
        in_specs=[pl.BlockSpec(memory_space=pltpu.SMEM), pl.BlockSpec((rows, lanes), lambda: (0, 0))],
        out_specs=pl.BlockSpec((ROT_DIM, rows, lanes), lambda: (0, 0, 0)),
        out_shape=jax.ShapeDtypeStruct((ROT_DIM, rows, lanes), F32),
    )(inv_freq, pos_dense)


def _qkv_kernel(x_ref, cs_ref, g_ref, w_ref, b_ref, qg_ref, kg_ref, expand_ref, ones_ref,
                q_ref, k_ref, v_ref):
    qd = N_HEADS * HEAD_DIM
    kd = N_KV_HEADS * HEAD_DIM
    h = _rms_norm(x_ref[...], g_ref[...]).astype(BF16)

    def project(c0, width):
        return jnp.dot(h, w_ref[:, c0:c0 + width], preferred_element_type=F32) + b_ref[:, c0:c0 + width]

    lanes = expand_ref.shape[1] // 3
    cs_hi, cs_lo, _ = _split3(cs_ref[...])
    tab = (jnp.dot(cs_hi, expand_ref[...], preferred_element_type=F32)
           + jnp.dot(cs_lo, expand_ref[...], preferred_element_type=F32))
    dim = lax.broadcasted_iota(jnp.int32, (1, lanes), 1) % HEAD_DIM
    cos = tab[:, :lanes] + (dim >= ROT_DIM).astype(F32)
    sin = tab[:, 2 * lanes:] - tab[:, lanes:2 * lanes]
    half = ROT_DIM // 2
    src = lax.broadcasted_iota(jnp.int32, (lanes, lanes), 0)
    dst = lax.broadcasted_iota(jnp.int32, (lanes, lanes), 1)
    dd = dst % HEAD_DIM
    partner = (jnp.where((dd < half) & (src == dst + half), -1.0, 0.0)
               + jnp.where((dd >= half) & (dd < ROT_DIM) & (src == dst - half), 1.0, 0.0)).astype(BF16)

    def norm_rope(c0, gain, scale, dst_ref, head0):
        z = project(c0, blk)
        yield
        ss = jnp.dot((z * z).astype(BF16), ones_ref[...], preferred_element_type=F32)
        yield
        z = z * lax.rsqrt(ss * (1.0 / HEAD_DIM) + NORM_EPS) * gain
        for j, c0 in enumerate(range(0, z.shape[1], lanes)):
            zz = z[:, c0:c0 + lanes]
            swapped = jnp.dot(zz.astype(BF16), partner, preferred_element_type=F32)
            rot = (zz * cos + swapped * sin) * scale
            dst_ref[0, head0 + 2 * j] = rot[:, :HEAD_DIM].astype(BF16)
            dst_ref[0, head0 + 2 * j + 1] = rot[:, HEAD_DIM:].astype(BF16)
            yield

    def values():
        vt = project(qd + kd, kd).T
        yield
        for j in range(N_KV_HEADS):
            v_ref[0, j] = vt[j * HEAD_DIM:(j + 1) * HEAD_DIM, :].astype(BF16)
        yield

    blk = 4 * HEAD_DIM
    live = [norm_rope(c0, qg_ref[...], HEAD_DIM ** -0.5 * LOG2_E, q_ref, c0 // HEAD_DIM) for c0 in range(0, qd, blk)]
    live += [norm_rope(qd, kg_ref[...], 1.0, k_ref, 0), values()]
    while live:
        live = [u for u in live if next(u, StopIteration) is not StopIteration]


def _qkv(x2, cs, g, w, b, qg, kg, expand, ones, bsz, seq, tt):
    n, d = x2.shape
    tps = seq // tt
    consts = [g, w, b, qg, kg, expand, ones]
    hm = lambda nh: pl.BlockSpec((1, nh, tt, HEAD_DIM), lambda i: (i // tps, 0, i % tps, 0))
    return pl.pallas_call(
        _qkv_kernel,
        grid=(n // tt,),
        in_specs=[pl.BlockSpec((tt, d), lambda i: (i, 0)), pl.BlockSpec((tt, ROT_DIM), lambda i: (i, 0))]
                 + [_const_spec(c.shape) for c in consts],
        out_specs=[hm(N_HEADS), hm(N_KV_HEADS),
                   pl.BlockSpec((1, N_KV_HEADS, HEAD_DIM, tt), lambda i: (i // tps, 0, 0, i % tps))],
        out_shape=[jax.ShapeDtypeStruct((bsz, N_HEADS, seq, HEAD_DIM), BF16),
                   jax.ShapeDtypeStruct((bsz, N_KV_HEADS, seq, HEAD_DIM), BF16),
                   jax.ShapeDtypeStruct((bsz, N_KV_HEADS, HEAD_DIM, seq), BF16)],
        compiler_params=_params(),
    )(x2, cs, *consts)


def _attn_kernel(sink_ref, q_ref, kc_ref, kp_ref, vc_ref, vp_ref, x_ref, w_ref, b_ref, out_ref, *, q_blocks):
    seq_start = pl.program_id(1) == 0
    blk = WINDOW
    cols = GROUP * blk
    c = lax.broadcasted_iota(jnp.int32, (2 * blk, cols), 0)
    t = lax.broadcasted_iota(jnp.int32, (2 * blk, cols), 1) % blk
    d = c - t
    band = (d >= 1) & (d <= blk)
    head_of_lane = lax.broadcasted_iota(jnp.int32, (1, cols), 1) // blk
    def unit(qb, g, outs):
        valid = band & ((c >= blk) | jnp.logical_not(seq_start)) if qb == 0 else band
        q = q_ref[0, g * GROUP:(g + 1) * GROUP, qb * blk:(qb + 1) * blk, :].reshape(cols, HEAD_DIM)
        if qb == 0:
            kprev, vprev = kp_ref[0, g], vp_ref[0, g]
        else:
            kprev = kc_ref[0, g, (qb - 1) * blk:qb * blk, :]
            vprev = vc_ref[0, g, :, (qb - 1) * blk:qb * blk]
        k = jnp.concatenate([kprev, kc_ref[0, g, qb * blk:(qb + 1) * blk, :]], axis=0)
        vt = jnp.concatenate([vprev, vc_ref[0, g, :, qb * blk:(qb + 1) * blk]], axis=1)
        s = lax.dot_general(k, q, NT_DIMS, preferred_element_type=F32)
        yield
        s = jnp.where(valid, s, -jnp.inf)
        sink = jnp.zeros((1, cols), F32)
        for j in range(GROUP):
            sink = jnp.where(head_of_lane == j, sink_ref[g * GROUP + j] * LOG2_E, sink)
        m = jnp.maximum(jnp.max(s, axis=0, keepdims=True), sink)
        p = jnp.exp2(s - m)
        den = jnp.sum(p, axis=0, keepdims=True) + jnp.exp2(sink - m)
        yield
        ot = jnp.dot(vt, p.astype(BF16), preferred_element_type=F32) / den
        yield
        o = ot.T
        outs[g] = [o[j * blk:(j + 1) * blk] for j in range(GROUP)]
        yield

    blocks = []
    for qb0 in range(0, q_blocks, ATTN_TOGETHER):
        qbs = range(qb0, min(qb0 + ATTN_TOGETHER, q_blocks))
        outs = {qb: [None] * N_KV_HEADS for qb in qbs}
        live = [unit(qb, g, outs[qb]) for qb in qbs for g in range(N_KV_HEADS)]
        while live:
            live = [u for u in live if next(u, StopIteration) is not StopIteration]
        for qb in qbs:
            blocks.append(jnp.concatenate([o for group in outs[qb] for o in group], axis=1).astype(BF16))
    o_all = jnp.concatenate(blocks, axis=0)
    out_ref[...] = x_ref[...] + jnp.dot(o_all, w_ref[...], preferred_element_type=F32) + b_ref[...]


def _attention(sinks, q, k, v, x2, w_o, b_o, q_blocks):
    bsz, _, seq, _ = q.shape
    d = x2.shape[1]
    tq = q_blocks * WINDOW
    steps = seq // tq
    kern = functools.partial(_attn_kernel, q_blocks=q_blocks)
    cur = lambda nh: pl.BlockSpec((1, nh, tq, HEAD_DIM), lambda b, i: (b, 0, i, 0))
    prev = pl.BlockSpec((1, N_KV_HEADS, WINDOW, HEAD_DIM), lambda b, i: (b, 0, jnp.maximum(i * q_blocks - 1, 0), 0))
    rows = pl.BlockSpec((tq, d), lambda b, i: (b * steps + i, 0))
    vcur = pl.BlockSpec((1, N_KV_HEADS, HEAD_DIM, tq), lambda b, i: (b, 0, 0, i))
    vprev = pl.BlockSpec((1, N_KV_HEADS, HEAD_DIM, WINDOW), lambda b, i: (b, 0, 0, jnp.maximum(i * q_blocks - 1, 0)))
    return pl.pallas_call(
        kern,
        grid=(bsz, steps),
        in_specs=[pl.BlockSpec(memory_space=pltpu.SMEM), cur(N_HEADS), cur(N_KV_HEADS), prev, vcur, vprev,
                  rows, _const_spec(w_o.shape), _const_spec(b_o.shape)],
        out_specs=rows,
        out_shape=jax.ShapeDtypeStruct((bsz * seq, d), F32),
        compiler_params=_params(2),
    )(sinks, q, k, k, v, v, x2, w_o, b_o)


def _block_ones(width, group):
    idx = np.arange(width) // group
    return jnp.asarray(idx[:, None] == idx[None, :], BF16)


def _chunk_tril(rows):
    idx = np.arange(rows)
    same = (idx[:, None] // CHUNK) == (idx[None, :] // CHUNK)
    return jnp.asarray(same & (idx[None, :] <= idx[:, None]), BF16)


def _rope_expand(lanes=128):
    half = ROT_DIM // 2
    dim = np.arange(lanes) % HEAD_DIM
    f = np.arange(half)[:, None]
    hit = (dim[None, :] % half == f)
    zero = np.zeros((half, lanes))
    cos_rows = np.concatenate([hit & (dim < ROT_DIM)[None, :], zero, zero], axis=1)
    sin_rows = np.concatenate([zero, -1.0 * (hit & (dim < half)[None, :]),
                               hit & ((dim >= half) & (dim < ROT_DIM))[None, :]], axis=1)
    return jnp.asarray(np.concatenate([cos_rows, sin_rows], axis=0), BF16)


def _row(v):
    return v.reshape(1, -1)


def kernel(x, positions, ab_norm_g, ab_w_in, conv_in_b, conv_dw_w, conv_dw_b, conv_ln_g, conv_ln_b, rwkv_mu, rwkv_w0, rwkv_w2, rwkv_a0, rwkv_a2, rwkv_g2, rwkv_k_k, rwkv_k_a, rwkv_r_k, rwkv_ln_g, rwkv_ln_b, ab_w_out, attn_norm_g, attn_w_qkv, attn_b_qkv, attn_q_norm_g, attn_k_norm_g, attn_sinks, attn_w_o, attn_b_o, ffn_norm_g, ffn_w_up, ffn_conv_w, ffn_conv_b, ffn_w_down):
    bsz, seq, d = x.shape
    depth = ffn_norm_g.shape[0]
    n = bsz * seq
    tt = 512
    ffn_tt = 512
    qkv_tt = 1024
    x2 = x.reshape(n, d)
    ones_rwkv = _block_ones(RWKV_W, RWKV_HEAD_DIM)
    ones_quad = _block_ones(4 * HEAD_DIM, HEAD_DIM)
    tri = _chunk_tril(256)
    rope_expand = _rope_expand()
    ffn_g = ffn_norm_g[:, None, :]
    ffn_cb = ffn_conv_b[:, None, :]
    ffn_up = ffn_w_up.astype(BF16)
    ffn_down = ffn_w_down.astype(BF16)

    for layer in range(depth):
        i = layer // 2
        if layer % 2 == 0:
            zeros = jnp.zeros((LORA_DECAY, RWKV_W), F32)
            w2a2 = jnp.concatenate([jnp.concatenate([rwkv_w2[i], zeros], axis=1),
                                    jnp.concatenate([zeros, rwkv_a2[i]], axis=1)], axis=0).astype(BF16)
            c, ar, bk, vv, wc, gate, bonus = _front(
                x2, _row(ab_norm_g[i]), ab_w_in[i].astype(BF16),
                _row(conv_in_b[i]), conv_dw_w[i], _row(conv_dw_b[i]), _row(conv_ln_g[i]), _row(conv_ln_b[i]),
                _row(rwkv_mu[i]), _row(rwkv_w0[i]), w2a2, _row(rwkv_a0[i]), rwkv_g2[i].astype(BF16),
                _row(rwkv_k_k[i]), _row(rwkv_k_a[i]), _row(rwkv_r_k[i]), ones_rwkv, tri, bsz, seq, tt)
            y = _rwkv_core(ar, bk, vv, wc)
            x2 = _mix_ffn(x2, c, y, gate, bonus, _row(rwkv_ln_g[i]), _row(rwkv_ln_b[i]), ones_quad,
                          ab_w_out[i].astype(BF16), ffn_g, ffn_up, ffn_conv_w, ffn_cb, ffn_down, layer, seq, ffn_tt)
            continue
        else:
            half = ROT_DIM // 2
            inv_freq = ROPE_THETA ** (-(jnp.arange(half, dtype=F32) * 2.0) / ROT_DIM)
            lanes = 128
            cs = _rope_table(inv_freq, positions.astype(F32).reshape(n // lanes, lanes))
            cs = cs.reshape(ROT_DIM, n).T
            q, k, v = _qkv(x2, cs, _row(attn_norm_g[i]), attn_w_qkv[i].astype(BF16), _row(attn_b_qkv[i]),
                           _row(jnp.tile(attn_q_norm_g[i], 4)), _row(jnp.tile(attn_k_norm_g[i], 4)),
                           rope_expand, ones_quad, bsz, seq, qkv_tt)
            x2 = _attention(attn_sinks[i], q, k, v, x2, attn_w_o[i].astype(BF16), _row(attn_b_o[i]), 8)
        x2 = _ffn(x2, ffn_g, ffn_up, ffn_conv_w, ffn_cb, ffn_down, layer, seq, ffn_tt)
    return x2.reshape(bsz, seq, d)
```
